```python
import functools
import jax, jax.numpy as jnp
from jax import lax
import numpy as np

D_MODEL = 4096
BATCH = 16
SEQ = 256
DEPTH = 4
DEC_BATCH = 4
DEC_SEQ = 4096
PAST_LEN = 256

GRID_W = 64
W_BRANCH = D_MODEL // 4
MIX = 4 * W_BRANCH
N_IN_SLICES = 14
IN_COLS = N_IN_SLICES * W_BRANCH
CONF_K = 31
SHORT_K = 3
CHUNK = 128
N_HEADS_GMLP = 8
GMLP_HEAD_DIM = W_BRANCH // N_HEADS_GMLP
N_HEADS_ATT = 8
HEAD_DIM = W_BRANCH // N_HEADS_ATT
WIN_R = 8
WIN_C = 16
QCB = 16
BAND = QCB + WIN_C
QBLK = 128
EPS = 1e-6
NEG = -1e30

kernel_name = 'hymba_style_diffusion_trunk_step'


def rmsnorm(x, g):
    x32 = x.astype(jnp.float32)
    y = x32 * lax.rsqrt(jnp.mean(x32 * x32, axis=-1, keepdims=True) + EPS)
    return (y * g.astype(jnp.float32)).astype(x.dtype)


def layernorm(x, g, b):
    x32 = x.astype(jnp.float32)
    mu = jnp.mean(x32, axis=-1, keepdims=True)
    var = jnp.mean(jnp.square(x32 - mu), axis=-1, keepdims=True)
    y = (x32 - mu) * lax.rsqrt(var + EPS)
    return (y * g.astype(jnp.float32) + b.astype(jnp.float32)).astype(x.dtype)


def dwconv(x, w):
    c = x.shape[-1]
    return lax.conv_general_dilated(
        x, w[:, None, :].astype(x.dtype), window_strides=(1,), padding='SAME',
        dimension_numbers=('NWC', 'WIO', 'NWC'), feature_group_count=c)


def conformer_conv(a_val, a_glu, conv_w, conv_b, ln_g, ln_b):
    y = a_val * jax.nn.sigmoid(a_glu)
    y = dwconv(y, conv_w) + conv_b
    return jax.nn.silu(layernorm(y, ln_g, ln_b))


def chunk_gmlp(u, v, ln_g, ln_b, ws, bias):
    bn, l, w = v.shape
    vn = layernorm(v, ln_g, ln_b).reshape(bn, l // CHUNK, CHUNK, N_HEADS_GMLP, GMLP_HEAD_DIM)
    s = jnp.einsum('hts,bnshd->bnthd', ws.astype(vn.dtype), vn) + bias.T[None, None, :, :, None]
    return u * s.reshape(bn, l, w)


def short_conv(bg, cg, xc, conv_w):
    return bg * dwconv(cg * xc, conv_w)


def context_attention(q, k, v):
    bn, l, h, hd = q.shape
    qb = (q * (HEAD_DIM ** -0.5)).reshape(bn, l // QBLK, QBLK, h, hd).transpose(1, 0, 2, 3, 4)

    def blk(qi):
        s = jnp.einsum('bqhd,bkhd->bhqk', qi, k).astype(jnp.float32)
        p = jax.nn.softmax(s, axis=-1).astype(v.dtype)
        return jnp.einsum('bhqk,bkhd->bqhd', p, v)

    o = lax.map(blk, qb)
    return o.transpose(1, 0, 2, 3, 4).reshape(bn, l, h * hd)


def _column_band():
    ncb = GRID_W // QCB
    j = np.arange(ncb)
    b0 = np.clip(j * QCB - WIN_C // 2, 0, GRID_W - BAND)
    col_idx = b0[:, None] + np.arange(BAND)[None, :]
    qc = j[:, None] * QCB + np.arange(QCB)[None, :]
    sc = np.clip(qc - WIN_C // 2, 0, GRID_W - WIN_C)
    kc = col_idx[:, None, :]
    valid = (kc >= sc[..., None]) & (kc < sc[..., None] + WIN_C)
    dc = np.clip(kc - qc[..., None] + WIN_C - 1, 0, 2 * WIN_C - 2)
    return col_idx.astype(np.int32), dc.astype(np.int32), valid


def neighborhood_attention(q, k, v, ctx_k, ctx_v, rpb):
    bn, l, h, hd = q.shape
    rows = l // GRID_W
    kr = min(WIN_R, rows)
    ncb = GRID_W // QCB
    col_idx, dc_idx, valid_np = _column_band()
    valid = jnp.asarray(valid_np)[:, :, None, :]
    rpb_dc = rpb[:, :, dc_idx]
    qg = (q * (HEAD_DIM ** -0.5)).reshape(bn, rows, GRID_W, h, hd)
    kg = k.reshape(bn, rows, GRID_W, h, hd)
    vg = v.reshape(bn, rows, GRID_W, h, hd)
    row_ids = jnp.arange(rows, dtype=jnp.int32)
    starts = jnp.clip(row_ids - kr // 2, 0, rows - kr)

    def row_fn(args):
        q_r, r, s_r = args
        k_rows = lax.dynamic_slice_in_dim(kg, s_r, kr, axis=1)
        v_rows = lax.dynamic_slice_in_dim(vg, s_r, kr, axis=1)
        k_band = k_rows[:, :, col_idx]
        v_band = v_rows[:, :, col_idx]
        qb = q_r.reshape(bn, ncb, QCB, h, hd)
        s_loc = jnp.einsum('bjqhd,brjkhd->bhjqrk', qb, k_band).astype(jnp.float32)
        dr = s_r + jnp.arange(kr, dtype=jnp.int32) - r + (WIN_R - 1)
        bias = jnp.take(rpb_dc, dr, axis=1).transpose(0, 2, 3, 1, 4)
        s_loc = jnp.where(valid, s_loc + bias.astype(jnp.float32), NEG)
        s_loc = s_loc.reshape(bn, h, ncb, QCB, kr * BAND)
        s_ctx = jnp.einsum('bjqhd,bchd->bhjqc', qb, ctx_k).astype(jnp.float32)
        p = jax.nn.softmax(jnp.concatenate([s_loc, s_ctx], axis=-1), axis=-1).astype(v.dtype)
        p_loc = p[..., :kr * BAND].reshape(bn, h, ncb, QCB, kr, BAND)
        p_ctx = p[..., kr * BAND:]
        o = (jnp.einsum('bhjqrk,brjkhd->bjqhd', p_loc, v_band)
             + jnp.einsum('bhjqc,bchd->bjqhd', p_ctx, ctx_v))
        return o.reshape(bn, GRID_W, h * hd)

    out = lax.map(row_fn, (qg.transpose(1, 0, 2, 3, 4), row_ids, starts))
    return out.transpose(1, 0, 2, 3).reshape(bn, l, h * hd)


def ada_modulation(cond, w, b):
    m = jax.nn.silu(cond) @ w + b
    return jnp.split(m, 3, axis=-1)


def mixer_sublayer(x, shift, scale, gate, g_pre, g_post, w_in, a_conv_w, a_conv_b, a_ln_g, a_ln_b,
                   b_ln_g, b_ln_b, b_ws, b_bias, c_conv_w, w_out, attend):
    bn, l, _ = x.shape
    hn = rmsnorm(x, g_pre) * (1 + scale) + shift
    (a_val, a_glu, a_gate, b_u, b_v, b_gate, c_b, c_c, c_x, c_gate,
     d_q, d_k, d_v, d_gate) = jnp.split(hn @ w_in, N_IN_SLICES, axis=-1)
    ya = conformer_conv(a_val, a_glu, a_conv_w, a_conv_b, a_ln_g, a_ln_b)
    yb = chunk_gmlp(b_u, b_v, b_ln_g, b_ln_b, b_ws, b_bias)
    yc = short_conv(c_b, c_c, c_x, c_conv_w)
    k = d_k.reshape(bn, l, N_HEADS_ATT, HEAD_DIM)
    v = d_v.reshape(bn, l, N_HEADS_ATT, HEAD_DIM)
    yd = attend(d_q.reshape(bn, l, N_HEADS_ATT, HEAD_DIM), k, v)
    y = jnp.concatenate([ya * jax.nn.silu(a_gate), yb * jax.nn.silu(b_gate),
                         yc * jax.nn.silu(c_gate), yd * jax.nn.silu(d_gate)], axis=-1) @ w_out
    return x + gate * rmsnorm(y, g_post), k, v


def setup_inputs(seed: int = 0) -> dict:
    key = jax.random.key(seed)
    ks = jax.random.split(key, 24)
    f32 = jnp.float32
    nrm = lambda k, shape, s: jax.random.normal(k, shape, f32) * s
    return {
        'x_prompt': nrm(ks[0], (BATCH, SEQ, D_MODEL), 1.0),
        'x_sample': nrm(ks[1], (DEC_BATCH, DEC_SEQ, D_MODEL), 1.0),
        'cache_k': nrm(ks[2], (DEC_BATCH, DEPTH, PAST_LEN, N_HEADS_ATT, HEAD_DIM), 1.0),
        'cache_v': nrm(ks[3], (DEC_BATCH, DEPTH, PAST_LEN, N_HEADS_ATT, HEAD_DIM), 1.0),
        'c': nrm(ks[4], (DEC_BATCH, D_MODEL), 1.0),
        'c_ctx': nrm(ks[5], (D_MODEL,), 1.0),
        'w_ada': nrm(ks[6], (DEPTH, D_MODEL, 3 * D_MODEL), 0.5 * D_MODEL ** -0.5),
        'b_ada': nrm(ks[7], (DEPTH, 3 * D_MODEL), 0.02),
        'g_pre': 1.0 + nrm(ks[8], (DEPTH, D_MODEL), 0.1),
        'g_post': 1.0 + nrm(ks[9], (DEPTH, D_MODEL), 0.1),
        'w_in': nrm(ks[10], (DEPTH, D_MODEL, IN_COLS), D_MODEL ** -0.5),
        'a_conv_w': nrm(ks[11], (DEPTH, CONF_K, W_BRANCH), CONF_K ** -0.5),
        'a_conv_b': nrm(ks[12], (DEPTH, W_BRANCH), 0.02),
        'a_ln_g': 1.0 + nrm(ks[13], (DEPTH, W_BRANCH), 0.1),
        'a_ln_b': nrm(ks[14], (DEPTH, W_BRANCH), 0.02),
        'b_ln_g': 1.0 + nrm(ks[15], (DEPTH, W_BRANCH), 0.1),
        'b_ln_b': nrm(ks[16], (DEPTH, W_BRANCH), 0.02),
        'b_ws': nrm(ks[17], (DEPTH, N_HEADS_GMLP, CHUNK, CHUNK), CHUNK ** -0.5),
        'b_bias': nrm(ks[18], (DEPTH, N_HEADS_GMLP, CHUNK), 0.02),
        'c_conv_w': nrm(ks[19], (DEPTH, SHORT_K, W_BRANCH), SHORT_K ** -0.5),
        'd_rpb': nrm(ks[20], (DEPTH, N_HEADS_ATT, 2 * WIN_R - 1, 2 * WIN_C - 1), 0.1),
        'w_out': nrm(ks[21], (DEPTH, MIX, D_MODEL), MIX ** -0.5),
    }


def reference(x_prompt, x_sample, cache_k, cache_v, c, c_ctx, w_ada, b_ada, g_pre, g_post, w_in,
              a_conv_w, a_conv_b, a_ln_g, a_ln_b, b_ln_g, b_ln_b, b_ws, b_bias, c_conv_w, d_rpb,
              w_out):
    y_prompt = x_prompt
    y_sample = x_sample
    new_ks = []
    new_vs = []
    for l in range(DEPTH):
        shared = (g_pre[l], g_post[l], w_in[l], a_conv_w[l], a_conv_b[l], a_ln_g[l], a_ln_b[l],
                  b_ln_g[l], b_ln_b[l], b_ws[l], b_bias[l], c_conv_w[l], w_out[l])
        sh, sc, gt = ada_modulation(c_ctx, w_ada[l], b_ada[l])
        y_prompt, k_l, v_l = mixer_sublayer(y_prompt, sh, sc, gt, *shared, attend=context_attention)
        new_ks.append(k_l)
        new_vs.append(v_l)
        sh, sc, gt = ada_modulation(c, w_ada[l], b_ada[l])
        attend_lat = functools.partial(neighborhood_attention, ctx_k=cache_k[:, l],
                                       ctx_v=cache_v[:, l], rpb=d_rpb[l])
        y_sample, _, _ = mixer_sublayer(y_sample, sh[:, None, :], sc[:, None, :], gt[:, None, :],
                                        *shared, attend=attend_lat)
    new_k = jnp.stack(new_ks, axis=1)
    new_v = jnp.stack(new_vs, axis=1)
    return (y_prompt, y_sample, new_k, new_v)
```

```python
import functools

import jax
import jax.numpy as jnp
import numpy as np
from jax import lax
from jax.experimental import pallas as pl
from jax.experimental.pallas import tpu as pltpu

GRID_W = 64
EPS = 1e-6
NEG = -1e30
N_IN_SLICES = 14
HALO = 16
VMEM_LIMIT_BYTES = 56 * 1024 * 1024
MM_TILE = 1024
SEQ_TILE = 512
CONV_ROWS = 64
LANES = 128
SUBLANES = 8

F32 = jnp.float32
BF16 = jnp.bfloat16


def _tile(n, target, unit=128):
    if n <= target:
        return n
    t = target - target % unit
    while n % t:
        t -= unit
    return t


def _params(*sem):
    return pltpu.CompilerParams(dimension_semantics=sem, vmem_limit_bytes=VMEM_LIMIT_BYTES)


def _silu(x):
    return x * jax.nn.sigmoid(x)


def _rms(x, g):
    return x * lax.rsqrt(jnp.mean(x * x, axis=-1, keepdims=True) + EPS) * g


def _ln(x, g, b):
    mu = jnp.mean(x, axis=-1, keepdims=True)
    xc = x - mu
    var = jnp.mean(xc * xc, axis=-1, keepdims=True)
    return xc * lax.rsqrt(var + EPS) * g + b


def _ada_kernel(cond_ref, w_ref, b_ref, o_ref):
    a = _silu(cond_ref[...]).astype(BF16)
    o_ref[...] = jnp.dot(a, w_ref[...].astype(BF16), preferred_element_type=F32) + b_ref[...]


def _ada_modulation(cond, w_ada, b_ada):
    depth, d, n = w_ada.shape
    r = cond.shape[0]
    tn = _tile(n, 512)
    return pl.pallas_call(
        _ada_kernel,
        grid=(depth, n // tn),
        in_specs=[pl.BlockSpec((r, d), lambda l, j: (0, 0)),
                  pl.BlockSpec((None, d, tn), lambda l, j: (l, 0, j)),
                  pl.BlockSpec((None, 1, tn), lambda l, j: (l, 0, j))],
        out_specs=pl.BlockSpec((None, r, tn), lambda l, j: (l, 0, j)),
        out_shape=jax.ShapeDtypeStruct((depth, r, n), F32),
        compiler_params=_params("arbitrary", "arbitrary"),
        name="ada_modulation",
    )(cond, w_ada, b_ada.reshape(depth, 1, n))


def _prenorm_kernel(x_ref, g_ref, mod_ref, o_ref):
    y = _rms(x_ref[...], g_ref[...])
    o_ref[...] = (y * (1.0 + mod_ref[1]) + mod_ref[0]).astype(o_ref.dtype)


def _post_kernel(x_ref, y_ref, gpost_ref, mod_ref, *rest, has_next):
    xn = x_ref[...] + mod_ref[2] * _rms(y_ref[...], gpost_ref[...])
    if has_next:
        gpre_ref, modn_ref, xo_ref, hn_ref = rest
        hn_ref[...] = (_rms(xn, gpre_ref[...]) * (1.0 + modn_ref[1]) + modn_ref[0]).astype(hn_ref.dtype)
    else:
        (xo_ref,) = rest
    xo_ref[...] = xn


def _row_tile(t):
    return min(256, t)


def _vec_spec(layer, width):
    return pl.BlockSpec((None, 1, width), lambda *_: (layer, 0, 0))


def _mod_spec(layer, row0, tiles_per_row, d):
    return pl.BlockSpec((None, None, 3, 1, d), lambda i: (layer, row0 + i // tiles_per_row, 0, 0, 0))


def _prenorm(x, g, mod, layer, row0, seq_len):
    t, d = x.shape
    tm = _row_tile(seq_len)
    return pl.pallas_call(
        _prenorm_kernel,
        grid=(t // tm,),
        in_specs=[pl.BlockSpec((tm, d), lambda i: (i, 0)),
                  _vec_spec(layer, d),
                  _mod_spec(layer, row0, seq_len // tm, d)],
        out_specs=pl.BlockSpec((tm, d), lambda i: (i, 0)),
        out_shape=jax.ShapeDtypeStruct((t, d), BF16),
        compiler_params=_params("arbitrary"),
        name="prenorm",
    )(x, g, mod)


def _post(x, y, g_post, g_pre, mod, layer, row0, seq_len, has_next):
    t, d = x.shape
    tm = _row_tile(seq_len)
    tpr = seq_len // tm
    row = pl.BlockSpec((tm, d), lambda i: (i, 0))
    in_specs = [row, row, _vec_spec(layer, d), _mod_spec(layer, row0, tpr, d)]
    args = [x, y, g_post, mod]
    out_specs = [row]
    out_shape = [jax.ShapeDtypeStruct((t, d), F32)]
    if has_next:
        in_specs += [_vec_spec(layer + 1, d), _mod_spec(layer + 1, row0, tpr, d)]
        args += [g_pre, mod]
        out_specs.append(row)
        out_shape.append(jax.ShapeDtypeStruct((t, d), BF16))
    out = pl.pallas_call(
        functools.partial(_post_kernel, has_next=has_next),
        grid=(t // tm,),
        in_specs=in_specs,
        out_specs=out_specs,
        out_shape=out_shape,
        compiler_params=_params("arbitrary"),
        name="post",
    )(*args)
    return (out[0], out[1]) if has_next else (out[0], None)


def _mm_kernel(x_ref, w_ref, o_ref):
    o_ref[...] = jnp.dot(x_ref[...], w_ref[...], preferred_element_type=F32).astype(o_ref.dtype)


def _in_proj(hn, w, layer, out_dtype):
    t, d = hn.shape
    n = w.shape[2]
    tm, tn = _tile(t, MM_TILE), _tile(n, MM_TILE)
    return pl.pallas_call(
        _mm_kernel,
        grid=(t // tm, n // tn),
        in_specs=[pl.BlockSpec((tm, d), lambda i, j: (i, 0)),
                  pl.BlockSpec((None, d, tn), lambda i, j: (layer, 0, j))],
        out_specs=pl.BlockSpec((tm, tn), lambda i, j: (i, j)),
        out_shape=jax.ShapeDtypeStruct((t, n), out_dtype),
        compiler_params=_params("arbitrary", "arbitrary"),
        name="in_proj",
    )(hn, w)


def _out_proj_kernel(a_ref, b_ref, c_ref, d_ref, w_ref, o_ref, cat_ref):
    wb = a_ref.shape[1]

    @pl.when(pl.program_id(1) == 0)
    def _():
        for s, ref in enumerate((a_ref, b_ref, c_ref, d_ref)):
            cat_ref[:, s * wb:(s + 1) * wb] = ref[...]

    o_ref[...] = jnp.dot(cat_ref[...], w_ref[...], preferred_element_type=F32)


def _out_proj(branches, w, layer):
    t, wb = branches[0].shape
    k, n = w.shape[1:]
    tm, tn = _tile(t, MM_TILE), _tile(n, MM_TILE)
    slab = pl.BlockSpec((tm, wb), lambda i, j: (i, 0))
    return pl.pallas_call(
        _out_proj_kernel,
        grid=(t // tm, n // tn),
        in_specs=[slab, slab, slab, slab, pl.BlockSpec((None, k, tn), lambda i, j: (layer, 0, j))],
        out_specs=pl.BlockSpec((tm, tn), lambda i, j: (i, j)),
        out_shape=jax.ShapeDtypeStruct((t, n), F32),
        scratch_shapes=[pltpu.VMEM((tm, k), BF16)],
        compiler_params=_params("arbitrary", "arbitrary"),
        name="out_proj",
    )(*branches, w)


def _seq_specs(tl, wb, total_rows, cols):
    r = tl // HALO
    last = total_rows // HALO - 1
    main = pl.BlockSpec((tl, wb), lambda i: (i, cols))
    prev = pl.BlockSpec((HALO, wb), lambda i: (jnp.maximum(i * r - 1, 0), cols))
    nxt = pl.BlockSpec((HALO, wb), lambda i: (jnp.minimum((i + 1) * r, last), cols))
    return main, prev, nxt


def _fill_halo(z_ref, fn, main, prev, nxt, tl, tiles_per_seq):
    i = pl.program_id(0) % tiles_per_seq
    z_ref[0:HALO, :] = jnp.where(i != 0, fn(*[r[...].astype(F32) for r in prev]), 0.0)
    z_ref[HALO:HALO + tl, :] = fn(*[r[...].astype(F32) for r in main])
    z_ref[HALO + tl:, :] = jnp.where(i != tiles_per_seq - 1, fn(*[r[...].astype(F32) for r in nxt]), 0.0)


def _conformer_kernel(val, val_p, val_n, glu, glu_p, glu_n, gate_ref, cw_ref, cb_ref, g_ref, b_ref, o_ref, z_ref,
                      y_ref, *, tl, tiles_per_seq):
    _fill_halo(z_ref, lambda v, g: v * jax.nn.sigmoid(g), (val, glu), (val_p, glu_p), (val_n, glu_n), tl, tiles_per_seq)
    taps = cw_ref.shape[0]
    wb = o_ref.shape[1]
    first = HALO - (taps - 1) // 2
    rc = min(CONV_ROWS, tl)

    def chunk(ci, carry):
        r0 = pl.multiple_of(ci * rc, rc)
        for cb in range(wb // LANES):
            lanes = slice(cb * LANES, (cb + 1) * LANES)
            win = z_ref[pl.ds(r0, rc + 2 * HALO), lanes]
            acc = jnp.broadcast_to(cb_ref[:, lanes], (rc, LANES))
            for s in range(SUBLANES):
                offs = [o for o in range(first, first + taps) if o % SUBLANES == s]
                if offs:
                    shifted = win[s:s + rc + offs[-1] - s]
                    for o in offs:
                        acc = acc + cw_ref[o - first:o - first + 1, lanes] * shifted[o - s:o - s + rc]
            y_ref[pl.ds(r0, rc), lanes] = acc
        y = _silu(_ln(y_ref[pl.ds(r0, rc), :], g_ref[...], b_ref[...]))
        o_ref[pl.ds(r0, rc), :] = (y * _silu(gate_ref[pl.ds(r0, rc), :].astype(F32))).astype(o_ref.dtype)
        return carry

    lax.fori_loop(0, tl // rc, chunk, 0)


def _branch_a(p, seq_len, conv_w, conv_b, ln_g, ln_b, layer):
    t = p.shape[0]
    wb = p.shape[1] // N_IN_SLICES
    tl = _tile(seq_len, SEQ_TILE)
    taps = conv_w.shape[1]
    v_specs = _seq_specs(tl, wb, t, 0)
    g_specs = _seq_specs(tl, wb, t, 1)
    vec = _vec_spec(layer, wb)
    return pl.pallas_call(
        functools.partial(_conformer_kernel, tl=tl, tiles_per_seq=seq_len // tl),
        grid=(t // tl,),
        in_specs=[*v_specs, *g_specs, pl.BlockSpec((tl, wb), lambda i: (i, 2)),
                  pl.BlockSpec((None, taps, wb), lambda i: (layer, 0, 0)), vec, vec, vec],
        out_specs=pl.BlockSpec((tl, wb), lambda i: (i, 0)),
        out_shape=jax.ShapeDtypeStruct((t, wb), BF16),
        scratch_shapes=[pltpu.VMEM((tl + 2 * HALO, wb), F32), pltpu.VMEM((tl, wb), F32)],
        compiler_params=_params("arbitrary"),
        name="branch_a_conformer",
    )(p, p, p, p, p, p, p, conv_w, conv_b, ln_g, ln_b)


def _gmlp_kernel(u_ref, v_ref, gate_ref, g_ref, b_ref, ws_ref, bias_ref, o_ref, *, tl):
    nh, chunk = ws_ref.shape[0], ws_ref.shape[1]
    hd = o_ref.shape[1] // nh
    vn = _ln(v_ref[...].astype(F32), g_ref[...], b_ref[...]).astype(BF16)
    for c in range(tl // chunk):
        rows = slice(c * chunk, (c + 1) * chunk)
        for h in range(nh):
            cols = slice(h * hd, (h + 1) * hd)
            s = jnp.dot(ws_ref[h], vn[rows, cols], preferred_element_type=F32) + bias_ref[:, cols]
            y = u_ref[rows, cols].astype(F32) * s * _silu(gate_ref[rows, cols].astype(F32))
            o_ref[rows, cols] = y.astype(o_ref.dtype)


def _branch_b(p, seq_len, ln_g, ln_b, ws, bias_full, layer):
    t = p.shape[0]
    wb = p.shape[1] // N_IN_SLICES
    nh, chunk = ws.shape[1], ws.shape[2]
    tl = _tile(seq_len, SEQ_TILE)
    vec = _vec_spec(layer, wb)
    return pl.pallas_call(
        functools.partial(_gmlp_kernel, tl=tl),
        grid=(t // tl,),
        in_specs=[pl.BlockSpec((tl, wb), lambda i: (i, 3)), pl.BlockSpec((tl, wb), lambda i: (i, 4)),
                  pl.BlockSpec((tl, wb), lambda i: (i, 5)), vec, vec,
                  pl.BlockSpec((None, nh, chunk, chunk), lambda i: (layer, 0, 0, 0)),
                  pl.BlockSpec((None, chunk, wb), lambda i: (layer, 0, 0))],
        out_specs=pl.BlockSpec((tl, wb), lambda i: (i, 0)),
        out_shape=jax.ShapeDtypeStruct((t, wb), BF16),
        compiler_params=_params("arbitrary"),
        name="branch_b_gmlp",
    )(p, p, p, ln_g, ln_b, ws, bias_full)


def _short_conv_kernel(cc, cc_p, cc_n, cx, cx_p, cx_n, cb_ref, gate_ref, cw_ref, o_ref, z_ref, *, tl, tiles_per_seq):
    _fill_halo(z_ref, lambda a, b: a * b, (cc, cx), (cc_p, cx_p), (cc_n, cx_n), tl, tiles_per_seq)
    taps = cw_ref.shape[0]
    first = HALO - (taps - 1) // 2
    acc = cw_ref[0:1, :] * z_ref[first:first + tl, :]
    for k in range(1, taps):
        acc = acc + cw_ref[k:k + 1, :] * z_ref[first + k:first + k + tl, :]
    o_ref[...] = (cb_ref[...].astype(F32) * acc * _silu(gate_ref[...].astype(F32))).astype(o_ref.dtype)


def _branch_c(p, seq_len, conv_w, layer):
    t = p.shape[0]
    wb = p.shape[1] // N_IN_SLICES
    tl = _tile(seq_len, SEQ_TILE)
    taps = conv_w.shape[1]
    return pl.pallas_call(
        functools.partial(_short_conv_kernel, tl=tl, tiles_per_seq=seq_len // tl),
        grid=(t // tl,),
        in_specs=[*_seq_specs(tl, wb, t, 7), *_seq_specs(tl, wb, t, 8),
                  pl.BlockSpec((tl, wb), lambda i: (i, 6)), pl.BlockSpec((tl, wb), lambda i: (i, 9)),
                  pl.BlockSpec((None, taps, wb), lambda i: (layer, 0, 0))],
        out_specs=pl.BlockSpec((tl, wb), lambda i: (i, 0)),
        out_shape=jax.ShapeDtypeStruct((t, wb), BF16),
        scratch_shapes=[pltpu.VMEM((tl + 2 * HALO, wb), F32)],
        compiler_params=_params("arbitrary"),
        name="branch_c_short_conv",
    )(p, p, p, p, p, p, p, p, conv_w)


_NT = (((1,), (1,)), ((), ()))


def _ctx_attn_kernel(q_ref, k_ref, v_ref, gate_ref, o_ref, *, nh):
    hd = o_ref.shape[1] // nh
    scale = hd ** -0.5
    for h in range(nh):
        cols = slice(h * hd, (h + 1) * hd)
        q = q_ref[:, cols].astype(BF16)
        s = lax.dot_general(q, k_ref[:, cols].astype(BF16), _NT, preferred_element_type=F32) * scale
        e = jnp.exp(s - jnp.max(s, axis=-1, keepdims=True))
        o = jnp.dot(e.astype(BF16), v_ref[:, cols].astype(BF16), preferred_element_type=F32)
        o = o / jnp.sum(e, axis=-1, keepdims=True)
        o_ref[:, cols] = (o * _silu(gate_ref[:, cols].astype(F32))).astype(o_ref.dtype)


def _branch_d_context(p, seq_len, nh):
    t = p.shape[0]
    wb = p.shape[1] // N_IN_SLICES
    return pl.pallas_call(
        functools.partial(_ctx_attn_kernel, nh=nh),
        grid=(t // seq_len,),
        in_specs=[pl.BlockSpec((seq_len, wb), lambda b, s=s: (b, s)) for s in (10, 11, 12, 13)],
        out_specs=pl.BlockSpec((seq_len, wb), lambda b: (b, 0)),
        out_shape=jax.ShapeDtypeStruct((t, wb), BF16),
        compiler_params=_params("arbitrary"),
        name="branch_d_context_attention",
    )(p, p, p, p)


def _nbr_attn_kernel(q_ref, k_ref, v_ref, gate_ref, ck_ref, cv_ref, bias_ref, o_ref, *, rows, kr):
    hd = o_ref.shape[1]
    scale = hd ** -0.5
    ck = ck_ref[...].astype(BF16)
    cv = cv_ref[...].astype(BF16)

    def row(r, carry):
        s_r = jnp.clip(r - kr // 2, 0, rows - kr)
        q0 = pl.multiple_of(r * GRID_W, GRID_W)
        k0 = pl.multiple_of(s_r * GRID_W, GRID_W)
        q = q_ref[pl.ds(q0, GRID_W), :].astype(BF16)
        kk = k_ref[pl.ds(k0, kr * GRID_W), :].astype(BF16)
        vv = v_ref[pl.ds(k0, kr * GRID_W), :].astype(BF16)
        s_loc = lax.dot_general(q, kk, _NT, preferred_element_type=F32) * scale + bias_ref[r - s_r]
        s_ctx = lax.dot_general(q, ck, _NT, preferred_element_type=F32) * scale
        m = jnp.maximum(jnp.max(s_loc, axis=-1, keepdims=True), jnp.max(s_ctx, axis=-1, keepdims=True))
        e_loc = jnp.exp(s_loc - m)
        e_ctx = jnp.exp(s_ctx - m)
        den = jnp.sum(e_loc, axis=-1, keepdims=True) + jnp.sum(e_ctx, axis=-1, keepdims=True)
        o = (jnp.dot(e_loc.astype(BF16), vv, preferred_element_type=F32)
             + jnp.dot(e_ctx.astype(BF16), cv, preferred_element_type=F32)) / den
        g = gate_ref[pl.ds(q0, GRID_W), :].astype(F32)
        o_ref[pl.ds(q0, GRID_W), :] = (o * _silu(g)).astype(o_ref.dtype)
        return carry

    lax.fori_loop(0, rows, row, 0)


def _branch_d_latent(p, seq_len, cache_k, cache_v, bias_tab, layer):
    t = p.shape[0]
    wb = p.shape[1] // N_IN_SLICES
    nh, kr = bias_tab.shape[1], bias_tab.shape[2]
    hd = wb // nh
    past = cache_k.shape[2]
    rows = seq_len // GRID_W
    col = lambda s: pl.BlockSpec((seq_len, hd), lambda b, h, s=s: (b, s * nh + h))
    ctx = pl.BlockSpec((None, None, past, hd), lambda b, h: (b, layer, 0, h))
    return pl.pallas_call(
        functools.partial(_nbr_attn_kernel, rows=rows, kr=kr),
        grid=(t // seq_len, nh),
        in_specs=[col(10), col(11), col(12), col(13), ctx, ctx,
                  pl.BlockSpec((None, None, kr, GRID_W, kr * GRID_W), lambda b, h: (layer, h, 0, 0, 0))],
        out_specs=pl.BlockSpec((seq_len, hd), lambda b, h: (b, h)),
        out_shape=jax.ShapeDtypeStruct((t, wb), BF16),
        compiler_params=_params("arbitrary", "arbitrary"),
        name="branch_d_neighbourhood_attention",
    )(p, p, p, p, cache_k, cache_v, bias_tab)


def _neighbourhood_bias(rpb, rows):
    win_r = (rpb.shape[2] + 1) // 2
    win_c = (rpb.shape[3] + 1) // 2
    kr = min(win_r, rows)
    qc = np.arange(GRID_W)[:, None]
    kc = np.arange(GRID_W)[None, :]
    sc = np.clip(qc - win_c // 2, 0, GRID_W - win_c)
    valid = (kc >= sc) & (kc < sc + win_c)
    dc = np.clip(kc - qc + win_c - 1, 0, 2 * win_c - 2)
    dr = np.arange(kr)[None, :] - np.arange(kr)[:, None] + win_r - 1
    tab = rpb[:, :, dr[:, None, :, None], dc[None, :, None, :]]
    tab = jnp.where(valid[None, None, None, :, None, :], tab, NEG)
    return tab.reshape(rpb.shape[0], rpb.shape[1], kr, GRID_W, kr * GRID_W).astype(F32)


def kernel(x_prompt, x_sample, cache_k, cache_v, c, c_ctx, w_ada, b_ada, g_pre, g_post, w_in, a_conv_w, a_conv_b,
           a_ln_g, a_ln_b, b_ln_g, b_ln_b, b_ws, b_bias, c_conv_w, d_rpb, w_out):
    batch, seq, d = x_prompt.shape
    dec_batch, dec_seq, _ = x_sample.shape
    depth = w_in.shape[0]
    wb = d // 4
    nh, hd = cache_k.shape[3], cache_k.shape[4]
    past = cache_k.shape[2]
    assert nh * hd == wb and w_in.shape[2] == N_IN_SLICES * wb and dec_seq % GRID_W == 0

    n_rows = -(-(1 + dec_batch) // 8) * 8
    cond = jnp.zeros((n_rows, d), F32).at[0].set(c_ctx).at[1:1 + dec_batch].set(c)
    mod = _ada_modulation(cond, w_ada, b_ada).reshape(depth, n_rows, 3, 1, d)

    vec = lambda a: a.reshape(depth, 1, a.shape[-1])
    g_pre, g_post, a_conv_b, a_ln_g, a_ln_b, b_ln_g, b_ln_b = map(
        vec, (g_pre, g_post, a_conv_b, a_ln_g, a_ln_b, b_ln_g, b_ln_b))
    w_in_b = w_in.astype(BF16)
    w_out_b = w_out.astype(BF16)
    ws_b = b_ws.astype(BF16)
    gh = wb // b_ws.shape[1]
    bias_full = jnp.repeat(jnp.swapaxes(b_bias, 1, 2), gh, axis=2)
    bias_tab = _neighbourhood_bias(d_rpb, dec_seq // GRID_W)
    ck = cache_k.reshape(dec_batch, depth, past, wb)
    cv = cache_v.reshape(dec_batch, depth, past, wb)

    xs = [x_prompt.reshape(batch * seq, d), x_sample.reshape(dec_batch * dec_seq, d)]
    seqs = [seq, dec_seq]
    row0 = [0, 1]
    mod_seq = [batch * seq, dec_seq]
    p_dtype = [F32, BF16]
    hn = [_prenorm(xs[i], g_pre, mod, 0, row0[i], mod_seq[i]) for i in range(2)]
    new_k, new_v = [], []
    for l in range(depth):
        for i in range(2):
            p = _in_proj(hn[i], w_in_b, l, p_dtype[i])
            ya = _branch_a(p, seqs[i], a_conv_w, a_conv_b, a_ln_g, a_ln_b, l)
            yb = _branch_b(p, seqs[i], b_ln_g, b_ln_b, ws_b, bias_full, l)
            yc = _branch_c(p, seqs[i], c_conv_w, l)
            if i == 0:
                yd = _branch_d_context(p, seq, nh)
                new_k.append(p[:, 11 * wb:12 * wb].reshape(batch, seq, nh, hd))
                new_v.append(p[:, 12 * wb:13 * wb].reshape(batch, seq, nh, hd))
            else:
                yd = _branch_d_latent(p, dec_seq, ck, cv, bias_tab, l)
            y = _out_proj((ya, yb, yc, yd), w_out_b, l)
            xs[i], hn[i] = _post(xs[i], y, g_post, g_pre, mod, l, row0[i], mod_seq[i], l + 1 < depth)
    return (xs[0].reshape(batch, seq, d), xs[1].reshape(dec_batch, dec_seq, d),
            jnp.stack(new_k, axis=1), jnp.stack(new_v, axis=1))
```

```python
import functools

import jax
import jax.numpy as jnp
import numpy as np
from jax import lax
from jax.experimental import pallas as pl
from jax.experimental.pallas import tpu as pltpu

GRID_W = 64
EPS = 1e-6
NEG = -1e30
N_IN_SLICES = 14
HALO = 16
VMEM_LIMIT_BYTES = 56 * 1024 * 1024
MM_TILE = 1024
SEQ_TILE = 512
CONV_ROWS = 64
ATTN_ROW_GROUP = 4
LANES = 128
SUBLANES = 8

F32 = jnp.float32
BF16 = jnp.bfloat16


def _tile(n, target, unit=128):
    if n <= target:
        return n
    t = target - target % unit
    while n % t:
        t -= unit
    return t


def _params(*sem):
    return pltpu.CompilerParams(dimension_semantics=sem, vmem_limit_bytes=VMEM_LIMIT_BYTES)


def _silu(x):
    return x * jax.nn.sigmoid(x)


def _rms(x, g):
    return x * lax.rsqrt(jnp.mean(x * x, axis=-1, keepdims=True) + EPS) * g


def _ln(x, g, b):
    mu = jnp.mean(x, axis=-1, keepdims=True)
    xc = x - mu
    var = jnp.mean(xc * xc, axis=-1, keepdims=True)
    return xc * lax.rsqrt(var + EPS) * g + b


def _ada_kernel(cond_ref, w_ref, b_ref, o_ref):
    a = _silu(cond_ref[...]).astype(BF16)
    o_ref[...] = jnp.dot(a, w_ref[...].astype(BF16), preferred_element_type=F32) + b_ref[...]


def _ada_modulation(cond, w_ada, b_ada):
    depth, d, n = w_ada.shape
    r = cond.shape[0]
    tn = _tile(n, 512)
    return pl.pallas_call(
        _ada_kernel,
        grid=(depth, n // tn),
        in_specs=[pl.BlockSpec((r, d), lambda l, j: (0, 0)),
                  pl.BlockSpec((None, d, tn), lambda l, j: (l, 0, j)),
                  pl.BlockSpec((None, 1, tn), lambda l, j: (l, 0, j))],
        out_specs=pl.BlockSpec((None, r, tn), lambda l, j: (l, 0, j)),
        out_shape=jax.ShapeDtypeStruct((depth, r, n), F32),
        compiler_params=_params("arbitrary", "arbitrary"),
        name="ada_modulation",
    )(cond, w_ada, b_ada.reshape(depth, 1, n))


def _prenorm_kernel(x_ref, g_ref, mod_ref, o_ref):
    y = _rms(x_ref[...], g_ref[...])
    o_ref[...] = (y * (1.0 + mod_ref[1]) + mod_ref[0]).astype(o_ref.dtype)


def _post_kernel(x_ref, y_ref, gpost_ref, mod_ref, *rest, has_next):
    xn = x_ref[...] + mod_ref[2] * _rms(y_ref[...], gpost_ref[...])
    if has_next:
        gpre_ref, modn_ref, xo_ref, hn_ref = rest
        hn_ref[...] = (_rms(xn, gpre_ref[...]) * (1.0 + modn_ref[1]) + modn_ref[0]).astype(hn_ref.dtype)
    else:
        (xo_ref,) = rest
    xo_ref[...] = xn


def _row_tile(t):
    return min(256, t)


def _vec_spec(layer, width):
    return pl.BlockSpec((None, 1, width), lambda *_: (layer, 0, 0))


def _mod_spec(layer, row0, tiles_per_row, d):
    return pl.BlockSpec((None, None, 3, 1, d), lambda i: (layer, row0 + i // tiles_per_row, 0, 0, 0))


def _prenorm(x, g, mod, layer, row0, seq_len):
    t, d = x.shape
    tm = _row_tile(seq_len)
    return pl.pallas_call(
        _prenorm_kernel,
        grid=(t // tm,),
        in_specs=[pl.BlockSpec((tm, d), lambda i: (i, 0)),
                  _vec_spec(layer, d),
                  _mod_spec(layer, row0, seq_len // tm, d)],
        out_specs=pl.BlockSpec((tm, d), lambda i: (i, 0)),
        out_shape=jax.ShapeDtypeStruct((t, d), BF16),
        compiler_params=_params("arbitrary"),
        name="prenorm",
    )(x, g, mod)


def _post(x, y, g_post, g_pre, mod, layer, row0, seq_len, has_next):
    t, d = x.shape
    tm = _row_tile(seq_len)
    tpr = seq_len // tm
    row = pl.BlockSpec((tm, d), lambda i: (i, 0))
    in_specs = [row, row, _vec_spec(layer, d), _mod_spec(layer, row0, tpr, d)]
    args = [x, y, g_post, mod]
    out_specs = [row]
    out_shape = [jax.ShapeDtypeStruct((t, d), F32)]
    if has_next:
        in_specs += [_vec_spec(layer + 1, d), _mod_spec(layer + 1, row0, tpr, d)]
        args += [g_pre, mod]
        out_specs.append(row)
        out_shape.append(jax.ShapeDtypeStruct((t, d), BF16))
    out = pl.pallas_call(
        functools.partial(_post_kernel, has_next=has_next),
        grid=(t // tm,),
        in_specs=in_specs,
        out_specs=out_specs,
        out_shape=out_shape,
        compiler_params=_params("arbitrary"),
        name="post",
    )(*args)
    return (out[0], out[1]) if has_next else (out[0], None)


def _mm_kernel(x_ref, w_ref, o_ref):
    o_ref[...] = jnp.dot(x_ref[...], w_ref[...], preferred_element_type=F32).astype(o_ref.dtype)


def _in_proj(hn, w, layer, out_dtype):
    t, d = hn.shape
    n = w.shape[2]
    tm, tn = _tile(t, MM_TILE), _tile(n, MM_TILE)
    return pl.pallas_call(
        _mm_kernel,
        grid=(t // tm, n // tn),
        in_specs=[pl.BlockSpec((tm, d), lambda i, j: (i, 0)),
                  pl.BlockSpec((None, d, tn), lambda i, j: (layer, 0, j))],
        out_specs=pl.BlockSpec((tm, tn), lambda i, j: (i, j)),
        out_shape=jax.ShapeDtypeStruct((t, n), out_dtype),
        compiler_params=_params("arbitrary", "arbitrary"),
        name="in_proj",
    )(hn, w)


def _out_proj_kernel(a_ref, b_ref, c_ref, d_ref, w_ref, o_ref, cat_ref):
    wb = a_ref.shape[1]

    @pl.when(pl.program_id(1) == 0)
    def _():
        for s, ref in enumerate((a_ref, b_ref, c_ref, d_ref)):
            cat_ref[:, s * wb:(s + 1) * wb] = ref[...]

    o_ref[...] = jnp.dot(cat_ref[...], w_ref[...], preferred_element_type=F32)


def _out_proj(branches, w, layer):
    t, wb = branches[0].shape
    k, n = w.shape[1:]
    tm, tn = _tile(t, MM_TILE), _tile(n, MM_TILE)
    slab = pl.BlockSpec((tm, wb), lambda i, j: (i, 0))
    return pl.pallas_call(
        _out_proj_kernel,
        grid=(t // tm, n // tn),
        in_specs=[slab, slab, slab, slab, pl.BlockSpec((None, k, tn), lambda i, j: (layer, 0, j))],
        out_specs=pl.BlockSpec((tm, tn), lambda i, j: (i, j)),
        out_shape=jax.ShapeDtypeStruct((t, n), F32),
        scratch_shapes=[pltpu.VMEM((tm, k), BF16)],
        compiler_params=_params("arbitrary", "arbitrary"),
        name="out_proj",
    )(*branches, w)


def _seq_specs(tl, wb, total_rows, cols):
    r = tl // HALO
    last = total_rows // HALO - 1
    main = pl.BlockSpec((tl, wb), lambda i: (i, cols))
    prev = pl.BlockSpec((HALO, wb), lambda i: (jnp.maximum(i * r - 1, 0), cols))
    nxt = pl.BlockSpec((HALO, wb), lambda i: (jnp.minimum((i + 1) * r, last), cols))
    return main, prev, nxt


def _fill_halo(z_ref, fn, main, prev, nxt, tl, tiles_per_seq):
    i = pl.program_id(0) % tiles_per_seq
    z_ref[0:HALO, :] = jnp.where(i != 0, fn(*[r[...].astype(F32) for r in prev]), 0.0)
    z_ref[HALO:HALO + tl, :] = fn(*[r[...].astype(F32) for r in main])
    z_ref[HALO + tl:, :] = jnp.where(i != tiles_per_seq - 1, fn(*[r[...].astype(F32) for r in nxt]), 0.0)


def _conformer_kernel(val, val_p, val_n, glu, glu_p, glu_n, gate_ref, cw_ref, cb_ref, g_ref, b_ref, o_ref, z_ref,
                      y_ref, *, tl, tiles_per_seq):
    _fill_halo(z_ref, lambda v, g: v * jax.nn.sigmoid(g), (val, glu), (val_p, glu_p), (val_n, glu_n), tl, tiles_per_seq)
    taps = cw_ref.shape[0]
    wb = o_ref.shape[1]
    first = HALO - (taps - 1) // 2
    rc = min(CONV_ROWS, tl)

    def chunk(ci, carry):
        r0 = pl.multiple_of(ci * rc, rc)
        for cb in range(wb // LANES):
            lanes = slice(cb * LANES, (cb + 1) * LANES)
            win = z_ref[pl.ds(r0, rc + 2 * HALO), lanes]
            acc = jnp.broadcast_to(cb_ref[:, lanes], (rc, LANES))
            for s in range(SUBLANES):
                offs = [o for o in range(first, first + taps) if o % SUBLANES == s]
                if offs:
                    rolled = pltpu.roll(win, win.shape[0] - s, 0) if s else win
                    for o in offs:
                        acc = acc + cw_ref[o - first:o - first + 1, lanes] * rolled[o - s:o - s + rc]
            y_ref[pl.ds(r0, rc), lanes] = acc
        y = _silu(_ln(y_ref[pl.ds(r0, rc), :], g_ref[...], b_ref[...]))
        o_ref[pl.ds(r0, rc), :] = (y * _silu(gate_ref[pl.ds(r0, rc), :].astype(F32))).astype(o_ref.dtype)
        return carry

    lax.fori_loop(0, tl // rc, chunk, 0)


def _branch_a(p, seq_len, conv_w, conv_b, ln_g, ln_b, layer):
    t = p.shape[0]
    wb = p.shape[1] // N_IN_SLICES
    tl = _tile(seq_len, SEQ_TILE)
    taps = conv_w.shape[1]
    v_specs = _seq_specs(tl, wb, t, 0)
    g_specs = _seq_specs(tl, wb, t, 1)
    vec = _vec_spec(layer, wb)
    return pl.pallas_call(
        functools.partial(_conformer_kernel, tl=tl, tiles_per_seq=seq_len // tl),
        grid=(t // tl,),
        in_specs=[*v_specs, *g_specs, pl.BlockSpec((tl, wb), lambda i: (i, 2)),
                  pl.BlockSpec((None, taps, wb), lambda i: (layer, 0, 0)), vec, vec, vec],
        out_specs=pl.BlockSpec((tl, wb), lambda i: (i, 0)),
        out_shape=jax.ShapeDtypeStruct((t, wb), BF16),
        scratch_shapes=[pltpu.VMEM((tl + 2 * HALO, wb), F32), pltpu.VMEM((tl, wb), F32)],
        compiler_params=_params("arbitrary"),
        name="branch_a_conformer",
    )(p, p, p, p, p, p, p, conv_w, conv_b, ln_g, ln_b)


def _gmlp_kernel(u_ref, v_ref, gate_ref, g_ref, b_ref, ws_ref, bias_ref, o_ref, *, tl):
    nh, chunk = ws_ref.shape[0], ws_ref.shape[1]
    hd = o_ref.shape[1] // nh
    vn = _ln(v_ref[...].astype(F32), g_ref[...], b_ref[...]).astype(BF16)
    for c in range(tl // chunk):
        rows = slice(c * chunk, (c + 1) * chunk)
        for h in range(nh):
            cols = slice(h * hd, (h + 1) * hd)
            s = jnp.dot(ws_ref[h], vn[rows, cols], preferred_element_type=F32) + bias_ref[:, cols]
            y = u_ref[rows, cols].astype(F32) * s * _silu(gate_ref[rows, cols].astype(F32))
            o_ref[rows, cols] = y.astype(o_ref.dtype)


def _branch_b(p, seq_len, ln_g, ln_b, ws, bias_full, layer):
    t = p.shape[0]
    wb = p.shape[1] // N_IN_SLICES
    nh, chunk = ws.shape[1], ws.shape[2]
    tl = _tile(seq_len, SEQ_TILE)
    vec = _vec_spec(layer, wb)
    return pl.pallas_call(
        functools.partial(_gmlp_kernel, tl=tl),
        grid=(t // tl,),
        in_specs=[pl.BlockSpec((tl, wb), lambda i: (i, 3)), pl.BlockSpec((tl, wb), lambda i: (i, 4)),
                  pl.BlockSpec((tl, wb), lambda i: (i, 5)), vec, vec,
                  pl.BlockSpec((None, nh, chunk, chunk), lambda i: (layer, 0, 0, 0)),
                  pl.BlockSpec((None, chunk, wb), lambda i: (layer, 0, 0))],
        out_specs=pl.BlockSpec((tl, wb), lambda i: (i, 0)),
        out_shape=jax.ShapeDtypeStruct((t, wb), BF16),
        compiler_params=_params("arbitrary"),
        name="branch_b_gmlp",
    )(p, p, p, ln_g, ln_b, ws, bias_full)


def _short_conv_kernel(cc, cc_p, cc_n, cx, cx_p, cx_n, cb_ref, gate_ref, cw_ref, o_ref, z_ref, *, tl, tiles_per_seq):
    _fill_halo(z_ref, lambda a, b: a * b, (cc, cx), (cc_p, cx_p), (cc_n, cx_n), tl, tiles_per_seq)
    taps = cw_ref.shape[0]
    first = HALO - (taps - 1) // 2
    acc = cw_ref[0:1, :] * z_ref[first:first + tl, :]
    for k in range(1, taps):
        acc = acc + cw_ref[k:k + 1, :] * z_ref[first + k:first + k + tl, :]
    o_ref[...] = (cb_ref[...].astype(F32) * acc * _silu(gate_ref[...].astype(F32))).astype(o_ref.dtype)


def _branch_c(p, seq_len, conv_w, layer):
    t = p.shape[0]
    wb = p.shape[1] // N_IN_SLICES
    tl = _tile(seq_len, SEQ_TILE)
    taps = conv_w.shape[1]
    return pl.pallas_call(
        functools.partial(_short_conv_kernel, tl=tl, tiles_per_seq=seq_len // tl),
        grid=(t // tl,),
        in_specs=[*_seq_specs(tl, wb, t, 7), *_seq_specs(tl, wb, t, 8),
                  pl.BlockSpec((tl, wb), lambda i: (i, 6)), pl.BlockSpec((tl, wb), lambda i: (i, 9)),
                  pl.BlockSpec((None, taps, wb), lambda i: (layer, 0, 0))],
        out_specs=pl.BlockSpec((tl, wb), lambda i: (i, 0)),
        out_shape=jax.ShapeDtypeStruct((t, wb), BF16),
        scratch_shapes=[pltpu.VMEM((tl + 2 * HALO, wb), F32)],
        compiler_params=_params("arbitrary"),
        name="branch_c_short_conv",
    )(p, p, p, p, p, p, p, p, conv_w)


_NT = (((1,), (1,)), ((), ()))


def _ctx_attn_kernel(q_ref, k_ref, v_ref, gate_ref, o_ref, *, nh):
    hd = o_ref.shape[1] // nh
    scale = hd ** -0.5
    for h in range(nh):
        cols = slice(h * hd, (h + 1) * hd)
        q = q_ref[:, cols].astype(BF16)
        s = lax.dot_general(q, k_ref[:, cols].astype(BF16), _NT, preferred_element_type=F32) * scale
        e = jnp.exp(s - jnp.max(s, axis=-1, keepdims=True))
        o = jnp.dot(e.astype(BF16), v_ref[:, cols].astype(BF16), preferred_element_type=F32)
        o = o / jnp.sum(e, axis=-1, keepdims=True)
        o_ref[:, cols] = (o * _silu(gate_ref[:, cols].astype(F32))).astype(o_ref.dtype)


def _branch_d_context(p, seq_len, nh):
    t = p.shape[0]
    wb = p.shape[1] // N_IN_SLICES
    return pl.pallas_call(
        functools.partial(_ctx_attn_kernel, nh=nh),
        grid=(t // seq_len,),
        in_specs=[pl.BlockSpec((seq_len, wb), lambda b, s=s: (b, s)) for s in (10, 11, 12, 13)],
        out_specs=pl.BlockSpec((seq_len, wb), lambda b: (b, 0)),
        out_shape=jax.ShapeDtypeStruct((t, wb), BF16),
        compiler_params=_params("arbitrary"),
        name="branch_d_context_attention",
    )(p, p, p, p)


def _nbr_attn_kernel(q_ref, k_ref, v_ref, gate_ref, ck_ref, cv_ref, bias_ref, o_ref, oc_ref, mc_ref, lc_ref,
                     *, rows, kr):
    hd = o_ref.shape[1]
    seq_len = q_ref.shape[0]
    scale = hd ** -0.5
    ck = ck_ref[...].astype(BF16)
    cv = cv_ref[...].astype(BF16)

    cq = _tile(seq_len, 512)

    def ctx_chunk(ci, carry):
        r0 = pl.multiple_of(ci * cq, cq)
        s = lax.dot_general(q_ref[pl.ds(r0, cq), :].astype(BF16), ck, _NT, preferred_element_type=F32) * scale
        m = jnp.max(s, axis=-1, keepdims=True)
        e = jnp.exp(s - m)
        mc_ref[pl.ds(r0, cq), :] = m
        lc_ref[pl.ds(r0, cq), :] = jnp.sum(e, axis=-1, keepdims=True)
        oc_ref[pl.ds(r0, cq), :] = jnp.dot(e.astype(BF16), cv, preferred_element_type=F32)
        return carry

    lax.fori_loop(0, seq_len // cq, ctx_chunk, 0)

    group = ATTN_ROW_GROUP if rows % ATTN_ROW_GROUP == 0 else 1

    def row_group(gi, carry):
        rs = [gi * group + i for i in range(group)]
        starts = [jnp.clip(r - kr // 2, 0, rows - kr) for r in rs]
        q0s = [pl.multiple_of(r * GRID_W, GRID_W) for r in rs]
        k0s = [pl.multiple_of(s * GRID_W, GRID_W) for s in starts]
        scores = [lax.dot_general(q_ref[pl.ds(q0, GRID_W), :].astype(BF16),
                                  k_ref[pl.ds(k0, kr * GRID_W), :].astype(BF16), _NT,
                                  preferred_element_type=F32) * scale + bias_ref[r - s]
                  for r, s, q0, k0 in zip(rs, starts, q0s, k0s)]
        m_l = [jnp.max(s, axis=-1, keepdims=True) for s in scores]
        e_l = [jnp.exp(s - m) for s, m in zip(scores, m_l)]
        l_l = [jnp.sum(e, axis=-1, keepdims=True) for e in e_l]
        o_l = [jnp.dot(e.astype(BF16), v_ref[pl.ds(k0, kr * GRID_W), :].astype(BF16), preferred_element_type=F32)
               for e, k0 in zip(e_l, k0s)]
        for i, q0 in enumerate(q0s):
            m_c = mc_ref[pl.ds(q0, GRID_W), :]
            m = jnp.maximum(m_l[i], m_c)
            a_l = jnp.exp(m_l[i] - m)
            a_c = jnp.exp(m_c - m)
            o = (a_l * o_l[i] + a_c * oc_ref[pl.ds(q0, GRID_W), :]) / (a_l * l_l[i] + a_c * lc_ref[pl.ds(q0, GRID_W), :])
            g = gate_ref[pl.ds(q0, GRID_W), :].astype(F32)
            o_ref[pl.ds(q0, GRID_W), :] = (o * _silu(g)).astype(o_ref.dtype)
        return carry

    lax.fori_loop(0, rows // group, row_group, 0)


def _branch_d_latent(p, seq_len, cache_k, cache_v, bias_tab, layer):
    t = p.shape[0]
    wb = p.shape[1] // N_IN_SLICES
    nh, kr = bias_tab.shape[1], bias_tab.shape[2]
    hd = wb // nh
    past = cache_k.shape[2]
    rows = seq_len // GRID_W
    col = lambda s: pl.BlockSpec((seq_len, hd), lambda b, h, s=s: (b, s * nh + h))
    ctx = pl.BlockSpec((None, None, past, hd), lambda b, h: (b, layer, 0, h))
    return pl.pallas_call(
        functools.partial(_nbr_attn_kernel, rows=rows, kr=kr),
        grid=(t // seq_len, nh),
        in_specs=[col(10), col(11), col(12), col(13), ctx, ctx,
                  pl.BlockSpec((None, None, kr, GRID_W, kr * GRID_W), lambda b, h: (layer, h, 0, 0, 0))],
        out_specs=pl.BlockSpec((seq_len, hd), lambda b, h: (b, h)),
        out_shape=jax.ShapeDtypeStruct((t, wb), BF16),
        scratch_shapes=[pltpu.VMEM((seq_len, hd), F32), pltpu.VMEM((seq_len, 1), F32), pltpu.VMEM((seq_len, 1), F32)],
        compiler_params=_params("arbitrary", "arbitrary"),
        name="branch_d_neighbourhood_attention",
    )(p, p, p, p, cache_k, cache_v, bias_tab)


def _neighbourhood_bias(rpb, rows):
    win_r = (rpb.shape[2] + 1) // 2
    win_c = (rpb.shape[3] + 1) // 2
    kr = min(win_r, rows)
    qc = np.arange(GRID_W)[:, None]
    kc = np.arange(GRID_W)[None, :]
    sc = np.clip(qc - win_c // 2, 0, GRID_W - win_c)
    valid = (kc >= sc) & (kc < sc + win_c)
    g = GRID_W
    padded = jnp.pad(rpb.astype(F32), ((0, 0), (0, 0), (0, 0), (g - win_c, g - win_c + 1)))
    flat = jnp.tile(padded, (1, 1, 1, g))
    cols = flat[..., g - 1:g - 1 + g * (2 * g - 1)].reshape(*rpb.shape[:3], g, 2 * g - 1)[..., :g]
    cols = jnp.where(valid, cols, NEG)
    tab = jnp.stack([cols[:, :, win_r - 1 - o:win_r - 1 - o + kr] for o in range(kr)], axis=2)
    tab = jnp.swapaxes(tab, 3, 4)
    return tab.reshape(rpb.shape[0], rpb.shape[1], kr, g, kr * g)


def kernel(x_prompt, x_sample, cache_k, cache_v, c, c_ctx, w_ada, b_ada, g_pre, g_post, w_in, a_conv_w, a_conv_b,
           a_ln_g, a_ln_b, b_ln_g, b_ln_b, b_ws, b_bias, c_conv_w, d_rpb, w_out):
    batch, seq, d = x_prompt.shape
    dec_batch, dec_seq, _ = x_sample.shape
    depth = w_in.shape[0]
    wb = d // 4
    nh, hd = cache_k.shape[3], cache_k.shape[4]
    past = cache_k.shape[2]
    assert nh * hd == wb and w_in.shape[2] == N_IN_SLICES * wb and dec_seq % GRID_W == 0

    n_rows = -(-(1 + dec_batch) // 8) * 8
    cond = jnp.zeros((n_rows, d), F32).at[0].set(c_ctx).at[1:1 + dec_batch].set(c)
    mod = _ada_modulation(cond, w_ada, b_ada).reshape(depth, n_rows, 3, 1, d)

    vec = lambda a: a.reshape(depth, 1, a.shape[-1])
    g_pre, g_post, a_conv_b, a_ln_g, a_ln_b, b_ln_g, b_ln_b = map(
        vec, (g_pre, g_post, a_conv_b, a_ln_g, a_ln_b, b_ln_g, b_ln_b))
    w_in_b = w_in.astype(BF16)
    w_out_b = w_out.astype(BF16)
    ws_b = b_ws.astype(BF16)
    gh = wb // b_ws.shape[1]
    bias_full = jnp.repeat(jnp.swapaxes(b_bias, 1, 2), gh, axis=2)
    bias_tab = _neighbourhood_bias(d_rpb, dec_seq // GRID_W)
    ck = cache_k.reshape(dec_batch, depth, past, wb)
    cv = cache_v.reshape(dec_batch, depth, past, wb)

    xs = [x_prompt.reshape(batch * seq, d), x_sample.reshape(dec_batch * dec_seq, d)]
    seqs = [seq, dec_seq]
    row0 = [0, 1]
    mod_seq = [batch * seq, dec_seq]
    p_dtype = [F32, BF16]
    hn = [_prenorm(xs[i], g_pre, mod, 0, row0[i], mod_seq[i]) for i in range(2)]
    new_k, new_v = [], []
    for l in range(depth):
        for i in range(2):
            p = _in_proj(hn[i], w_in_b, l, p_dtype[i])
            ya = _branch_a(p, seqs[i], a_conv_w, a_conv_b, a_ln_g, a_ln_b, l)
            yb = _branch_b(p, seqs[i], b_ln_g, b_ln_b, ws_b, bias_full, l)
            yc = _branch_c(p, seqs[i], c_conv_w, l)
            if i == 0:
                yd = _branch_d_context(p, seq, nh)
                new_k.append(p[:, 11 * wb:12 * wb].reshape(batch, seq, nh, hd))
                new_v.append(p[:, 12 * wb:13 * wb].reshape(batch, seq, nh, hd))
            else:
                yd = _branch_d_latent(p, dec_seq, ck, cv, bias_tab, l)
            y = _out_proj((ya, yb, yc, yd), w_out_b, l)
            xs[i], hn[i] = _post(xs[i], y, g_post, g_pre, mod, l, row0[i], mod_seq[i], l + 1 < depth)
    return (xs[0].reshape(batch, seq, d), xs[1].reshape(dec_batch, dec_seq, d),
            jnp.stack(new_k, axis=1), jnp.stack(new_v, axis=1))
```

```python
import functools

import jax
import jax.numpy as jnp
import numpy as np
from jax import lax
from jax.experimental import pallas as pl
from jax.experimental.pallas import tpu as pltpu

GRID_W = 64
EPS = 1e-6
NEG = -1e30
N_IN_SLICES = 14
HALO = 16
VMEM_LIMIT_BYTES = 56 * 1024 * 1024
MM_TILE = 1024
SEQ_TILE = 512
CONV_ROWS = 64
ATTN_ROW_GROUP = 4
LANES = 128
SUBLANES = 8

F32 = jnp.float32
BF16 = jnp.bfloat16


def _tile(n, target, unit=128):
    if n <= target:
        return n
    t = target - target % unit
    while n % t:
        t -= unit
    return t


def _params(*sem):
    return pltpu.CompilerParams(dimension_semantics=sem, vmem_limit_bytes=VMEM_LIMIT_BYTES)


def _silu(x):
    return x * jax.nn.sigmoid(x)


def _rms(x, g):
    return x * lax.rsqrt(jnp.mean(x * x, axis=-1, keepdims=True) + EPS) * g


def _ln(x, g, b):
    mu = jnp.mean(x, axis=-1, keepdims=True)
    xc = x - mu
    var = jnp.mean(xc * xc, axis=-1, keepdims=True)
    return xc * lax.rsqrt(var + EPS) * g + b


def _ada_kernel(cond_ref, w_ref, b_ref, o_ref):
    a = _silu(cond_ref[...]).astype(BF16)
    o_ref[...] = jnp.dot(a, w_ref[...].astype(BF16), preferred_element_type=F32) + b_ref[...]


def _ada_modulation(cond, w_ada, b_ada):
    depth, d, n = w_ada.shape
    r = cond.shape[0]
    tn = _tile(n, 512)
    return pl.pallas_call(
        _ada_kernel,
        grid=(depth, n // tn),
        in_specs=[pl.BlockSpec((r, d), lambda l, j: (0, 0)),
                  pl.BlockSpec((None, d, tn), lambda l, j: (l, 0, j)),
                  pl.BlockSpec((None, 1, tn), lambda l, j: (l, 0, j))],
        out_specs=pl.BlockSpec((None, r, tn), lambda l, j: (l, 0, j)),
        out_shape=jax.ShapeDtypeStruct((depth, r, n), F32),
        compiler_params=_params("arbitrary", "arbitrary"),
        name="ada_modulation",
    )(cond, w_ada, b_ada.reshape(depth, 1, n))


def _prenorm_kernel(x_ref, g_ref, mod_ref, o_ref):
    y = _rms(x_ref[...], g_ref[...])
    o_ref[...] = (y * (1.0 + mod_ref[1]) + mod_ref[0]).astype(o_ref.dtype)


def _post_kernel(x_ref, y_ref, gpost_ref, mod_ref, *rest, has_next):
    xn = x_ref[...] + mod_ref[2] * _rms(y_ref[...].astype(F32), gpost_ref[...])
    if has_next:
        gpre_ref, modn_ref, xo_ref, hn_ref = rest
        hn_ref[...] = (_rms(xn, gpre_ref[...]) * (1.0 + modn_ref[1]) + modn_ref[0]).astype(hn_ref.dtype)
    else:
        (xo_ref,) = rest
    xo_ref[...] = xn


def _row_tile(t):
    return min(256, t)


def _vec_spec(layer, width):
    return pl.BlockSpec((None, 1, width), lambda *_: (layer, 0, 0))


def _mod_spec(layer, row0, tiles_per_row, d):
    return pl.BlockSpec((None, None, 3, 1, d), lambda i: (layer, row0 + i // tiles_per_row, 0, 0, 0))


def _prenorm(x, g, mod, layer, row0, seq_len):
    t, d = x.shape
    tm = _row_tile(seq_len)
    return pl.pallas_call(
        _prenorm_kernel,
        grid=(t // tm,),
        in_specs=[pl.BlockSpec((tm, d), lambda i: (i, 0)),
                  _vec_spec(layer, d),
                  _mod_spec(layer, row0, seq_len // tm, d)],
        out_specs=pl.BlockSpec((tm, d), lambda i: (i, 0)),
        out_shape=jax.ShapeDtypeStruct((t, d), BF16),
        compiler_params=_params("arbitrary"),
        name="prenorm",
    )(x, g, mod)


def _post(x, y, g_post, g_pre, mod, layer, row0, seq_len, has_next):
    t, d = x.shape
    tm = _row_tile(seq_len)
    tpr = seq_len // tm
    row = pl.BlockSpec((tm, d), lambda i: (i, 0))
    in_specs = [row, row, _vec_spec(layer, d), _mod_spec(layer, row0, tpr, d)]
    args = [x, y, g_post, mod]
    out_specs = [row]
    out_shape = [jax.ShapeDtypeStruct((t, d), F32)]
    if has_next:
        in_specs += [_vec_spec(layer + 1, d), _mod_spec(layer + 1, row0, tpr, d)]
        args += [g_pre, mod]
        out_specs.append(row)
        out_shape.append(jax.ShapeDtypeStruct((t, d), BF16))
    out = pl.pallas_call(
        functools.partial(_post_kernel, has_next=has_next),
        grid=(t // tm,),
        in_specs=in_specs,
        out_specs=out_specs,
        out_shape=out_shape,
        compiler_params=_params("arbitrary"),
        name="post",
    )(*args)
    return (out[0], out[1]) if has_next else (out[0], None)


def _in_proj_kernel(x_ref, w_ref, o_ref, wb_ref):
    @pl.when(pl.program_id(1) == 0)
    def _():
        wb_ref[...] = w_ref[...].astype(BF16)

    o_ref[...] = jnp.dot(x_ref[...], wb_ref[...], preferred_element_type=F32).astype(o_ref.dtype)


def _in_proj(hn, w, layer, out_dtype):
    t, d = hn.shape
    n = w.shape[2]
    tm, tn = _tile(t, MM_TILE // 2), _tile(n, MM_TILE)
    return pl.pallas_call(
        _in_proj_kernel,
        grid=(n // tn, t // tm),
        in_specs=[pl.BlockSpec((tm, d), lambda j, i: (i, 0)),
                  pl.BlockSpec((None, d, tn), lambda j, i: (layer, 0, j))],
        out_specs=pl.BlockSpec((tm, tn), lambda j, i: (i, j)),
        out_shape=jax.ShapeDtypeStruct((t, n), out_dtype),
        scratch_shapes=[pltpu.VMEM((d, tn), BF16)],
        compiler_params=_params("arbitrary", "arbitrary"),
        name="in_proj",
    )(hn, w)


def _out_proj_kernel(a_ref, b_ref, c_ref, d_ref, w_ref, o_ref, cat_ref):
    wb = a_ref.shape[1]

    @pl.when(pl.program_id(1) == 0)
    def _():
        for s, ref in enumerate((a_ref, b_ref, c_ref, d_ref)):
            cat_ref[:, s * wb:(s + 1) * wb] = ref[...]

    o_ref[...] = jnp.dot(cat_ref[...], w_ref[...], preferred_element_type=F32).astype(o_ref.dtype)


def _out_proj(branches, w, layer):
    t, wb = branches[0].shape
    k, n = w.shape[1:]
    tm, tn = _tile(t, MM_TILE), _tile(n, MM_TILE)
    slab = pl.BlockSpec((tm, wb), lambda i, j: (i, 0))
    return pl.pallas_call(
        _out_proj_kernel,
        grid=(t // tm, n // tn),
        in_specs=[slab, slab, slab, slab, pl.BlockSpec((None, k, tn), lambda i, j: (layer, 0, j))],
        out_specs=pl.BlockSpec((tm, tn), lambda i, j: (i, j)),
        out_shape=jax.ShapeDtypeStruct((t, n), BF16),
        scratch_shapes=[pltpu.VMEM((tm, k), BF16)],
        compiler_params=_params("arbitrary", "arbitrary"),
        name="out_proj",
    )(*branches, w)


def _seq_specs(tl, wb, total_rows, cols):
    r = tl // HALO
    last = total_rows // HALO - 1
    main = pl.BlockSpec((tl, wb), lambda i: (i, cols))
    prev = pl.BlockSpec((HALO, wb), lambda i: (jnp.maximum(i * r - 1, 0), cols))
    nxt = pl.BlockSpec((HALO, wb), lambda i: (jnp.minimum((i + 1) * r, last), cols))
    return main, prev, nxt


def _fill_halo(z_ref, fn, main, prev, nxt, tl, tiles_per_seq):
    i = pl.program_id(0) % tiles_per_seq
    z_ref[0:HALO, :] = jnp.where(i != 0, fn(*[r[...].astype(F32) for r in prev]), 0.0)
    z_ref[HALO:HALO + tl, :] = fn(*[r[...].astype(F32) for r in main])
    z_ref[HALO + tl:, :] = jnp.where(i != tiles_per_seq - 1, fn(*[r[...].astype(F32) for r in nxt]), 0.0)


def _conv_block(z_ref, cw_ref, r0, rc, lanes):
    taps = cw_ref.shape[0]
    first = HALO - (taps - 1) // 2
    win = z_ref[pl.ds(r0, rc + 2 * HALO), lanes]
    acc = None
    for s in range(SUBLANES):
        offs = [o for o in range(first, first + taps) if o % SUBLANES == s]
        if offs:
            rolled = pltpu.roll(win, win.shape[0] - s, 0) if s else win
            for o in offs:
                term = cw_ref[o - first:o - first + 1, lanes] * rolled[o - s:o - s + rc]
                acc = term if acc is None else acc + term
    return acc


def _conformer_kernel(val, val_p, val_n, glu, glu_p, glu_n, gate_ref, cw_ref, cb_ref, g_ref, b_ref, o_ref, z_ref,
                      y_ref, *, tl, tiles_per_seq):
    _fill_halo(z_ref, lambda v, g: v * jax.nn.sigmoid(g), (val, glu), (val_p, glu_p), (val_n, glu_n), tl, tiles_per_seq)
    wb = o_ref.shape[1]
    rc = min(CONV_ROWS, tl)

    def chunk(ci, carry):
        r0 = pl.multiple_of(ci * rc, rc)
        for cb in range(wb // LANES):
            lanes = slice(cb * LANES, (cb + 1) * LANES)
            y_ref[pl.ds(r0, rc), lanes] = _conv_block(z_ref, cw_ref, r0, rc, lanes) + cb_ref[:, lanes]
        y = _silu(_ln(y_ref[pl.ds(r0, rc), :], g_ref[...], b_ref[...]))
        o_ref[pl.ds(r0, rc), :] = (y * _silu(gate_ref[pl.ds(r0, rc), :].astype(F32))).astype(o_ref.dtype)
        return carry

    lax.fori_loop(0, tl // rc, chunk, 0)


def _branch_a(p, seq_len, conv_w, conv_b, ln_g, ln_b, layer):
    t = p.shape[0]
    wb = p.shape[1] // N_IN_SLICES
    tl = _tile(seq_len, SEQ_TILE)
    taps = conv_w.shape[1]
    v_specs = _seq_specs(tl, wb, t, 0)
    g_specs = _seq_specs(tl, wb, t, 1)
    vec = _vec_spec(layer, wb)
    return pl.pallas_call(
        functools.partial(_conformer_kernel, tl=tl, tiles_per_seq=seq_len // tl),
        grid=(t // tl,),
        in_specs=[*v_specs, *g_specs, pl.BlockSpec((tl, wb), lambda i: (i, 2)),
                  pl.BlockSpec((None, taps, wb), lambda i: (layer, 0, 0)), vec, vec, vec],
        out_specs=pl.BlockSpec((tl, wb), lambda i: (i, 0)),
        out_shape=jax.ShapeDtypeStruct((t, wb), BF16),
        scratch_shapes=[pltpu.VMEM((tl + 2 * HALO, wb), F32), pltpu.VMEM((tl, wb), F32)],
        compiler_params=_params("arbitrary"),
        name="branch_a_conformer",
    )(p, p, p, p, p, p, p, conv_w, conv_b, ln_g, ln_b)


def _gmlp_kernel(u_ref, v_ref, gate_ref, g_ref, b_ref, ws_ref, bias_ref, o_ref, *, tl):
    nh, chunk = ws_ref.shape[0], ws_ref.shape[1]
    hd = o_ref.shape[1] // nh
    vn = _ln(v_ref[...].astype(F32), g_ref[...], b_ref[...]).astype(BF16)
    for c in range(tl // chunk):
        rows = slice(c * chunk, (c + 1) * chunk)
        for h in range(nh):
            cols = slice(h * hd, (h + 1) * hd)
            s = jnp.dot(ws_ref[h], vn[rows, cols], preferred_element_type=F32) + bias_ref[:, cols]
            y = u_ref[rows, cols].astype(F32) * s * _silu(gate_ref[rows, cols].astype(F32))
            o_ref[rows, cols] = y.astype(o_ref.dtype)


def _branch_b(p, seq_len, ln_g, ln_b, ws, bias_full, layer):
    t = p.shape[0]
    wb = p.shape[1] // N_IN_SLICES
    nh, chunk = ws.shape[1], ws.shape[2]
    tl = _tile(seq_len, SEQ_TILE)
    vec = _vec_spec(layer, wb)
    return pl.pallas_call(
        functools.partial(_gmlp_kernel, tl=tl),
        grid=(t // tl,),
        in_specs=[pl.BlockSpec((tl, wb), lambda i: (i, 3)), pl.BlockSpec((tl, wb), lambda i: (i, 4)),
                  pl.BlockSpec((tl, wb), lambda i: (i, 5)), vec, vec,
                  pl.BlockSpec((None, nh, chunk, chunk), lambda i: (layer, 0, 0, 0)),
                  pl.BlockSpec((None, chunk, wb), lambda i: (layer, 0, 0))],
        out_specs=pl.BlockSpec((tl, wb), lambda i: (i, 0)),
        out_shape=jax.ShapeDtypeStruct((t, wb), BF16),
        compiler_params=_params("arbitrary"),
        name="branch_b_gmlp",
    )(p, p, p, ln_g, ln_b, ws, bias_full)


def _short_conv_kernel(cc, cc_p, cc_n, cx, cx_p, cx_n, cb_ref, gate_ref, cw_ref, o_ref, z_ref, *, tl, tiles_per_seq):
    _fill_halo(z_ref, lambda a, b: a * b, (cc, cx), (cc_p, cx_p), (cc_n, cx_n), tl, tiles_per_seq)
    wb = o_ref.shape[1]
    rc = min(CONV_ROWS, tl)

    def chunk(ci, carry):
        r0 = pl.multiple_of(ci * rc, rc)
        for blk in range(wb // LANES):
            lanes = slice(blk * LANES, (blk + 1) * LANES)
            y = cb_ref[pl.ds(r0, rc), lanes].astype(F32) * _conv_block(z_ref, cw_ref, r0, rc, lanes)
            o_ref[pl.ds(r0, rc), lanes] = (y * _silu(gate_ref[pl.ds(r0, rc), lanes].astype(F32))).astype(o_ref.dtype)
        return carry

    lax.fori_loop(0, tl // rc, chunk, 0)


def _branch_c(p, seq_len, conv_w, layer):
    t = p.shape[0]
    wb = p.shape[1] // N_IN_SLICES
    tl = _tile(seq_len, SEQ_TILE)
    taps = conv_w.shape[1]
    return pl.pallas_call(
        functools.partial(_short_conv_kernel, tl=tl, tiles_per_seq=seq_len // tl),
        grid=(t // tl,),
        in_specs=[*_seq_specs(tl, wb, t, 7), *_seq_specs(tl, wb, t, 8),
                  pl.BlockSpec((tl, wb), lambda i: (i, 6)), pl.BlockSpec((tl, wb), lambda i: (i, 9)),
                  pl.BlockSpec((None, taps, wb), lambda i: (layer, 0, 0))],
        out_specs=pl.BlockSpec((tl, wb), lambda i: (i, 0)),
        out_shape=jax.ShapeDtypeStruct((t, wb), BF16),
        scratch_shapes=[pltpu.VMEM((tl + 2 * HALO, wb), F32)],
        compiler_params=_params("arbitrary"),
        name="branch_c_short_conv",
    )(p, p, p, p, p, p, p, p, conv_w)


_NT = (((1,), (1,)), ((), ()))


def _ctx_attn_kernel(q_ref, k_ref, v_ref, gate_ref, o_ref, *, nh):
    hd = o_ref.shape[1] // nh
    scale = hd ** -0.5
    for h in range(nh):
        cols = slice(h * hd, (h + 1) * hd)
        q = q_ref[:, cols].astype(BF16)
        s = lax.dot_general(q, k_ref[:, cols].astype(BF16), _NT, preferred_element_type=F32) * scale
        e = jnp.exp(s - jnp.max(s, axis=-1, keepdims=True))
        o = jnp.dot(e.astype(BF16), v_ref[:, cols].astype(BF16), preferred_element_type=F32)
        o = o / jnp.sum(e, axis=-1, keepdims=True)
        o_ref[:, cols] = (o * _silu(gate_ref[:, cols].astype(F32))).astype(o_ref.dtype)


def _branch_d_context(p, seq_len, nh):
    t = p.shape[0]
    wb = p.shape[1] // N_IN_SLICES
    return pl.pallas_call(
        functools.partial(_ctx_attn_kernel, nh=nh),
        grid=(t // seq_len,),
        in_specs=[pl.BlockSpec((seq_len, wb), lambda b, s=s: (b, s)) for s in (10, 11, 12, 13)],
        out_specs=pl.BlockSpec((seq_len, wb), lambda b: (b, 0)),
        out_shape=jax.ShapeDtypeStruct((t, wb), BF16),
        compiler_params=_params("arbitrary"),
        name="branch_d_context_attention",
    )(p, p, p, p)


def _nbr_attn_kernel(q_ref, k_ref, v_ref, gate_ref, ck_ref, cv_ref, bias_ref, o_ref, oc_ref, mc_ref, lc_ref,
                     *, rows, kr):
    hd = o_ref.shape[1]
    seq_len = q_ref.shape[0]
    win_r = bias_ref.shape[0] // 2 + 1
    scale = hd ** -0.5
    ck = ck_ref[...].astype(BF16)
    cv = cv_ref[...].astype(BF16)

    cq = _tile(seq_len, 512)

    def ctx_chunk(ci, carry):
        r0 = pl.multiple_of(ci * cq, cq)
        s = lax.dot_general(q_ref[pl.ds(r0, cq), :].astype(BF16), ck, _NT, preferred_element_type=F32) * scale
        m = jnp.max(s, axis=-1, keepdims=True)
        e = jnp.exp(s - m)
        mc_ref[pl.ds(r0, cq), :] = m
        lc_ref[pl.ds(r0, cq), :] = jnp.sum(e, axis=-1, keepdims=True)
        oc_ref[pl.ds(r0, cq), :] = jnp.dot(e.astype(BF16), cv, preferred_element_type=F32)
        return carry

    lax.fori_loop(0, seq_len // cq, ctx_chunk, 0, unroll=2 if (seq_len // cq) % 2 == 0 else 1)

    group = ATTN_ROW_GROUP if rows % ATTN_ROW_GROUP == 0 else 1

    def window_starts(gi):
        rs = [gi * group + i for i in range(group)]
        starts = [jnp.clip(r - kr // 2, 0, rows - kr) for r in rs]
        return rs, starts, [pl.multiple_of(s * GRID_W, GRID_W) for s in starts]

    def scores_stage(gi):
        rs, starts, k0s = window_starts(gi)
        out = []
        for r, s, k0 in zip(rs, starts, k0s):
            q0 = pl.multiple_of(r * GRID_W, GRID_W)
            base = win_r - 1 - (r - s)
            bias = jnp.concatenate([bias_ref[base + 2 * t] for t in range(kr // 2)], axis=1)
            out.append(lax.dot_general(q_ref[pl.ds(q0, GRID_W), :].astype(BF16),
                                       k_ref[pl.ds(k0, kr * GRID_W), :].astype(BF16), _NT,
                                       preferred_element_type=F32) * scale + bias)
        return tuple(out)

    def softmax_stage(scores):
        out = []
        for sc in scores:
            m = jnp.max(sc, axis=-1, keepdims=True)
            e = jnp.exp(sc - m)
            out.append((e.astype(BF16), m, jnp.sum(e, axis=-1, keepdims=True)))
        return tuple(out)

    def values_stage(gi, stats):
        rs, _, k0s = window_starts(gi)
        for r, k0, (e, m_l, l_l) in zip(rs, k0s, stats):
            q0 = pl.multiple_of(r * GRID_W, GRID_W)
            o_l = jnp.dot(e, v_ref[pl.ds(k0, kr * GRID_W), :].astype(BF16), preferred_element_type=F32)
            m_c = mc_ref[pl.ds(q0, GRID_W), :]
            m = jnp.maximum(m_l, m_c)
            a_l = jnp.exp(m_l - m)
            a_c = jnp.exp(m_c - m)
            o = (a_l * o_l + a_c * oc_ref[pl.ds(q0, GRID_W), :]) / (a_l * l_l + a_c * lc_ref[pl.ds(q0, GRID_W), :])
            g = gate_ref[pl.ds(q0, GRID_W), :].astype(F32)
            o_ref[pl.ds(q0, GRID_W), :] = (o * _silu(g)).astype(o_ref.dtype)

    def pipelined(gi, stats):
        nxt = softmax_stage(scores_stage(gi + 1))
        values_stage(gi, stats)
        return nxt

    n_groups = rows // group
    last = lax.fori_loop(0, n_groups - 1, pipelined, softmax_stage(scores_stage(jnp.int32(0))))
    values_stage(jnp.int32(n_groups - 1), last)


def _branch_d_latent(p, seq_len, cache_k, cache_v, bias_tab, layer):
    t = p.shape[0]
    wb = p.shape[1] // N_IN_SLICES
    nh, n_off = bias_tab.shape[1], bias_tab.shape[2]
    hd = wb // nh
    past = cache_k.shape[2]
    rows = seq_len // GRID_W
    kr = min(n_off // 2 + 1, rows)
    assert kr % 2 == 0
    col = lambda s: pl.BlockSpec((seq_len, hd), lambda b, h, s=s: (b, s * nh + h))
    ctx = pl.BlockSpec((None, None, past, hd), lambda b, h: (b, layer, 0, h))
    return pl.pallas_call(
        functools.partial(_nbr_attn_kernel, rows=rows, kr=kr),
        grid=(t // seq_len, nh),
        in_specs=[col(10), col(11), col(12), col(13), ctx, ctx,
                  pl.BlockSpec((None, None, n_off, GRID_W, 2 * GRID_W), lambda b, h: (layer, h, 0, 0, 0))],
        out_specs=pl.BlockSpec((seq_len, hd), lambda b, h: (b, h)),
        out_shape=jax.ShapeDtypeStruct((t, wb), BF16),
        scratch_shapes=[pltpu.VMEM((seq_len, hd), F32), pltpu.VMEM((seq_len, 1), F32), pltpu.VMEM((seq_len, 1), F32)],
        compiler_params=_params("arbitrary", "arbitrary"),
        name="branch_d_neighbourhood_attention",
    )(p, p, p, p, cache_k, cache_v, bias_tab)


def _neighbourhood_bias(rpb):
    win_c = (rpb.shape[3] + 1) // 2
    qc = np.arange(GRID_W)[:, None]
    kc = np.arange(GRID_W)[None, :]
    sc = np.clip(qc - win_c // 2, 0, GRID_W - win_c)
    valid = (kc >= sc) & (kc < sc + win_c)
    g = GRID_W
    padded = jnp.pad(rpb.astype(F32), ((0, 0), (0, 0), (0, 0), (g - win_c, g - win_c + 1)))
    flat = jnp.tile(padded, (1, 1, 1, g))
    cols = flat[..., g - 1:g - 1 + g * (2 * g - 1)].reshape(*rpb.shape[:3], g, 2 * g - 1)[..., :g]
    cols = jnp.where(valid, cols, NEG)
    return jnp.concatenate([cols[:, :, :-1], cols[:, :, 1:]], axis=-1)


def kernel(x_prompt, x_sample, cache_k, cache_v, c, c_ctx, w_ada, b_ada, g_pre, g_post, w_in, a_conv_w, a_conv_b,
           a_ln_g, a_ln_b, b_ln_g, b_ln_b, b_ws, b_bias, c_conv_w, d_rpb, w_out):
    batch, seq, d = x_prompt.shape
    dec_batch, dec_seq, _ = x_sample.shape
    depth = w_in.shape[0]
    wb = d // 4
    nh, hd = cache_k.shape[3], cache_k.shape[4]
    past = cache_k.shape[2]
    assert nh * hd == wb and w_in.shape[2] == N_IN_SLICES * wb and dec_seq % GRID_W == 0

    n_rows = -(-(1 + dec_batch) // 8) * 8
    cond = jnp.zeros((n_rows, d), F32).at[0].set(c_ctx).at[1:1 + dec_batch].set(c)
    mod = _ada_modulation(cond, w_ada, b_ada).reshape(depth, n_rows, 3, 1, d)

    vec = lambda a: a.reshape(depth, 1, a.shape[-1])
    g_pre, g_post, a_conv_b, a_ln_g, a_ln_b, b_ln_g, b_ln_b = map(
        vec, (g_pre, g_post, a_conv_b, a_ln_g, a_ln_b, b_ln_g, b_ln_b))
    w_out_b = w_out.astype(BF16)
    ws_b = b_ws.astype(BF16)
    gh = wb // b_ws.shape[1]
    bias_full = jnp.repeat(jnp.swapaxes(b_bias, 1, 2), gh, axis=2)
    bias_tab = _neighbourhood_bias(d_rpb)
    ck = cache_k.reshape(dec_batch, depth, past, wb)
    cv = cache_v.reshape(dec_batch, depth, past, wb)

    xs = [x_prompt.reshape(batch * seq, d), x_sample.reshape(dec_batch * dec_seq, d)]
    seqs = [seq, dec_seq]
    row0 = [0, 1]
    mod_seq = [batch * seq, dec_seq]
    p_dtype = [F32, BF16]
    hn = [_prenorm(xs[i], g_pre, mod, 0, row0[i], mod_seq[i]) for i in range(2)]
    new_k, new_v = [], []
    for l in range(depth):
        for i in range(2):
            p = _in_proj(hn[i], w_in, l, p_dtype[i])
            ya = _branch_a(p, seqs[i], a_conv_w, a_conv_b, a_ln_g, a_ln_b, l)
            yb = _branch_b(p, seqs[i], b_ln_g, b_ln_b, ws_b, bias_full, l)
            yc = _branch_c(p, seqs[i], c_conv_w, l)
            if i == 0:
                yd = _branch_d_context(p, seq, nh)
                new_k.append(p[:, 11 * wb:12 * wb].reshape(batch, seq, nh, hd))
                new_v.append(p[:, 12 * wb:13 * wb].reshape(batch, seq, nh, hd))
            else:
                yd = _branch_d_latent(p, dec_seq, ck, cv, bias_tab, l)
            y = _out_proj((ya, yb, yc, yd), w_out_b, l)
            xs[i], hn[i] = _post(xs[i], y, g_post, g_pre, mod, l, row0[i], mod_seq[i], l + 1 < depth)
    return (xs[0].reshape(batch, seq, d), xs[1].reshape(dec_batch, dec_seq, d),
            jnp.stack(new_k, axis=1), jnp.stack(new_v, axis=1))
```

```python
import functools

import jax
import jax.numpy as jnp
import numpy as np
from jax import lax
from jax.experimental import pallas as pl
from jax.experimental.pallas import tpu as pltpu

GRID_W = 64
EPS = 1e-6
NEG = -1e30
N_IN_SLICES = 14
HALO = 16
VMEM_LIMIT_BYTES = 56 * 1024 * 1024
MM_TILE = 1024
SEQ_TILE = 512
CONV_ROWS = 64
ATTN_ROW_GROUP = 4
LANES = 128
SUBLANES = 8

F32 = jnp.float32
BF16 = jnp.bfloat16


def _tile(n, target, unit=128):
    if n <= target:
        return n
    t = target - target % unit
    while n % t:
        t -= unit
    return t


def _params(*sem):
    return pltpu.CompilerParams(dimension_semantics=sem, vmem_limit_bytes=VMEM_LIMIT_BYTES)


def _silu(x):
    return x * jax.nn.sigmoid(x)


def _rms(x, g):
    return x * lax.rsqrt(jnp.mean(x * x, axis=-1, keepdims=True) + EPS) * g


def _ln(x, g, b):
    mu = jnp.mean(x, axis=-1, keepdims=True)
    xc = x - mu
    var = jnp.mean(xc * xc, axis=-1, keepdims=True)
    return xc * lax.rsqrt(var + EPS) * g + b


def _ada_kernel(cond_ref, w_ref, b_ref, o_ref):
    a = _silu(cond_ref[...]).astype(BF16)
    o_ref[...] = jnp.dot(a, w_ref[...].astype(BF16), preferred_element_type=F32) + b_ref[...]


def _ada_modulation(cond, w_ada, b_ada):
    depth, d, n = w_ada.shape
    r = cond.shape[0]
    tn = _tile(n, 512)
    return pl.pallas_call(
        _ada_kernel,
        grid=(depth, n // tn),
        in_specs=[pl.BlockSpec((r, d), lambda l, j: (0, 0)),
                  pl.BlockSpec((None, d, tn), lambda l, j: (l, 0, j)),
                  pl.BlockSpec((None, 1, tn), lambda l, j: (l, 0, j))],
        out_specs=pl.BlockSpec((None, r, tn), lambda l, j: (l, 0, j)),
        out_shape=jax.ShapeDtypeStruct((depth, r, n), F32),
        compiler_params=_params("arbitrary", "arbitrary"),
        name="ada_modulation",
    )(cond, w_ada, b_ada.reshape(depth, 1, n))


def _prenorm_kernel(x_ref, g_ref, mod_ref, o_ref):
    y = _rms(x_ref[...], g_ref[...])
    o_ref[...] = (y * (1.0 + mod_ref[1]) + mod_ref[0]).astype(o_ref.dtype)


def _post_kernel(x_ref, y_ref, gpost_ref, mod_ref, *rest, has_next):
    xn = x_ref[...] + mod_ref[2] * _rms(y_ref[...].astype(F32), gpost_ref[...])
    if has_next:
        gpre_ref, modn_ref, xo_ref, hn_ref = rest
        hn_ref[...] = (_rms(xn, gpre_ref[...]) * (1.0 + modn_ref[1]) + modn_ref[0]).astype(hn_ref.dtype)
    else:
        (xo_ref,) = rest
    xo_ref[...] = xn


def _row_tile(t):
    return min(256, t)


def _vec_spec(layer, width):
    return pl.BlockSpec((None, 1, width), lambda *_: (layer, 0, 0))


def _mod_spec(layer, row0, tiles_per_row, d):
    return pl.BlockSpec((None, None, 3, 1, d), lambda i: (layer, row0 + i // tiles_per_row, 0, 0, 0))


def _prenorm(x, g, mod, layer, row0, seq_len):
    t, d = x.shape
    tm = _row_tile(seq_len)
    return pl.pallas_call(
        _prenorm_kernel,
        grid=(t // tm,),
        in_specs=[pl.BlockSpec((tm, d), lambda i: (i, 0)),
                  _vec_spec(layer, d),
                  _mod_spec(layer, row0, seq_len // tm, d)],
        out_specs=pl.BlockSpec((tm, d), lambda i: (i, 0)),
        out_shape=jax.ShapeDtypeStruct((t, d), BF16),
        compiler_params=_params("arbitrary"),
        name="prenorm",
    )(x, g, mod)


def _post(x, y, g_post, g_pre, mod, layer, row0, seq_len, has_next):
    t, d = x.shape
    tm = _row_tile(seq_len)
    tpr = seq_len // tm
    row = pl.BlockSpec((tm, d), lambda i: (i, 0))
    in_specs = [row, row, _vec_spec(layer, d), _mod_spec(layer, row0, tpr, d)]
    args = [x, y, g_post, mod]
    out_specs = [row]
    out_shape = [jax.ShapeDtypeStruct((t, d), F32)]
    if has_next:
        in_specs += [_vec_spec(layer + 1, d), _mod_spec(layer + 1, row0, tpr, d)]
        args += [g_pre, mod]
        out_specs.append(row)
        out_shape.append(jax.ShapeDtypeStruct((t, d), BF16))
    out = pl.pallas_call(
        functools.partial(_post_kernel, has_next=has_next),
        grid=(t // tm,),
        in_specs=in_specs,
        out_specs=out_specs,
        out_shape=out_shape,
        compiler_params=_params("arbitrary"),
        name="post",
    )(*args)
    return (out[0], out[1]) if has_next else (out[0], None)


def _round_next_weights(rest):
    if len(rest) == 2:
        rest[1][...] = rest[0][...].astype(BF16)


def _next_weight_specs(w_all, next_layer, grid_rows, tn):
    k = w_all.shape[1]
    rows = k // grid_rows
    assert rows * grid_rows == k and rows % (2 * SUBLANES) == 0
    return (pl.BlockSpec((None, rows, tn), lambda i, j: (next_layer, i, j)),
            pl.BlockSpec((rows, tn), lambda i, j: (i, j)),
            jax.ShapeDtypeStruct(w_all.shape[1:], BF16))


def _in_proj_kernel(x_ref, w_ref, *rest):
    o_ref = rest[len(rest) // 2]
    o_ref[...] = jnp.dot(x_ref[...], w_ref[...], preferred_element_type=F32).astype(o_ref.dtype)
    _round_next_weights(rest[::2] if len(rest) == 3 else ())


def _in_proj(hn, w, out_dtype, w_all=None, next_layer=None):
    t, d = hn.shape
    n = w.shape[1]
    tm, tn = _tile(t, MM_TILE), _tile(n, MM_TILE)
    in_specs = [pl.BlockSpec((tm, d), lambda i, j: (i, 0)), pl.BlockSpec((d, tn), lambda i, j: (0, j))]
    out_specs = [pl.BlockSpec((tm, tn), lambda i, j: (i, j))]
    out_shape = [jax.ShapeDtypeStruct((t, n), out_dtype)]
    args = [hn, w]
    if w_all is not None:
        wn_in, wn_out, wn_shape = _next_weight_specs(w_all, next_layer, t // tm, tn)
        in_specs.append(wn_in)
        out_specs.append(wn_out)
        out_shape.append(wn_shape)
        args.append(w_all)
    out = pl.pallas_call(
        _in_proj_kernel,
        grid=(t // tm, n // tn),
        in_specs=in_specs,
        out_specs=out_specs,
        out_shape=out_shape,
        compiler_params=_params("arbitrary", "arbitrary"),
        name="in_proj",
    )(*args)
    return out if w_all is not None else out[0]


def _out_proj_kernel(a_ref, b_ref, c_ref, d_ref, w_ref, *rest):
    cat_ref = rest[-1]
    o_ref = rest[-3] if len(rest) == 4 else rest[0]
    wb = a_ref.shape[1]

    @pl.when(pl.program_id(1) == 0)
    def _():
        for s, ref in enumerate((a_ref, b_ref, c_ref, d_ref)):
            cat_ref[:, s * wb:(s + 1) * wb] = ref[...]

    o_ref[...] = jnp.dot(cat_ref[...], w_ref[...], preferred_element_type=F32).astype(o_ref.dtype)
    if len(rest) == 4:
        _round_next_weights((rest[0], rest[2]))


def _out_proj(branches, w, w_all=None, next_layer=None):
    t, wb = branches[0].shape
    k, n = w.shape
    tm, tn = _tile(t, MM_TILE), _tile(n, MM_TILE)
    slab = pl.BlockSpec((tm, wb), lambda i, j: (i, 0))
    in_specs = [slab, slab, slab, slab, pl.BlockSpec((k, tn), lambda i, j: (0, j))]
    out_specs = [pl.BlockSpec((tm, tn), lambda i, j: (i, j))]
    out_shape = [jax.ShapeDtypeStruct((t, n), BF16)]
    args = [*branches, w]
    if w_all is not None:
        wn_in, wn_out, wn_shape = _next_weight_specs(w_all, next_layer, t // tm, tn)
        in_specs.append(wn_in)
        out_specs.append(wn_out)
        out_shape.append(wn_shape)
        args.append(w_all)
    out = pl.pallas_call(
        _out_proj_kernel,
        grid=(t // tm, n // tn),
        in_specs=in_specs,
        out_specs=out_specs,
        out_shape=out_shape,
        scratch_shapes=[pltpu.VMEM((tm, k), BF16)],
        compiler_params=_params("arbitrary", "arbitrary"),
        name="out_proj",
    )(*args)
    return out if w_all is not None else out[0]


def _seq_specs(tl, wb, total_rows, cols):
    r = tl // HALO
    last = total_rows // HALO - 1
    main = pl.BlockSpec((tl, wb), lambda i: (i, cols))
    prev = pl.BlockSpec((HALO, wb), lambda i: (jnp.maximum(i * r - 1, 0), cols))
    nxt = pl.BlockSpec((HALO, wb), lambda i: (jnp.minimum((i + 1) * r, last), cols))
    return main, prev, nxt


def _fill_halo(z_ref, fn, main, prev, nxt, tl, tiles_per_seq):
    i = pl.program_id(0) % tiles_per_seq
    z_ref[0:HALO, :] = jnp.where(i != 0, fn(*[r[...].astype(F32) for r in prev]), 0.0)
    z_ref[HALO:HALO + tl, :] = fn(*[r[...].astype(F32) for r in main])
    z_ref[HALO + tl:, :] = jnp.where(i != tiles_per_seq - 1, fn(*[r[...].astype(F32) for r in nxt]), 0.0)


def _conv_block(z_ref, cw_ref, r0, rc, lanes):
    taps = cw_ref.shape[0]
    first = HALO - (taps - 1) // 2
    win = z_ref[pl.ds(r0, rc + 2 * HALO), lanes]
    acc = None
    for s in range(SUBLANES):
        offs = [o for o in range(first, first + taps) if o % SUBLANES == s]
        if offs:
            rolled = pltpu.roll(win, win.shape[0] - s, 0) if s else win
            for o in offs:
                term = cw_ref[o - first:o - first + 1, lanes] * rolled[o - s:o - s + rc]
                acc = term if acc is None else acc + term
    return acc


def _conformer_kernel(val, val_p, val_n, glu, glu_p, glu_n, gate_ref, cw_ref, cb_ref, g_ref, b_ref, o_ref, z_ref,
                      y_ref, *, tl, tiles_per_seq):
    _fill_halo(z_ref, lambda v, g: v * jax.nn.sigmoid(g), (val, glu), (val_p, glu_p), (val_n, glu_n), tl, tiles_per_seq)
    wb = o_ref.shape[1]
    rc = min(CONV_ROWS, tl)

    def chunk(ci, carry):
        r0 = pl.multiple_of(ci * rc, rc)
        for cb in range(wb // LANES):
            lanes = slice(cb * LANES, (cb + 1) * LANES)
            y_ref[pl.ds(r0, rc), lanes] = _conv_block(z_ref, cw_ref, r0, rc, lanes) + cb_ref[:, lanes]
        y = _silu(_ln(y_ref[pl.ds(r0, rc), :], g_ref[...], b_ref[...]))
        o_ref[pl.ds(r0, rc), :] = (y * _silu(gate_ref[pl.ds(r0, rc), :].astype(F32))).astype(o_ref.dtype)
        return carry

    lax.fori_loop(0, tl // rc, chunk, 0)


def _branch_a(p, seq_len, conv_w, conv_b, ln_g, ln_b, layer):
    t = p.shape[0]
    wb = p.shape[1] // N_IN_SLICES
    tl = _tile(seq_len, SEQ_TILE)
    taps = conv_w.shape[1]
    v_specs = _seq_specs(tl, wb, t, 0)
    g_specs = _seq_specs(tl, wb, t, 1)
    vec = _vec_spec(layer, wb)
    return pl.pallas_call(
        functools.partial(_conformer_kernel, tl=tl, tiles_per_seq=seq_len // tl),
        grid=(t // tl,),
        in_specs=[*v_specs, *g_specs, pl.BlockSpec((tl, wb), lambda i: (i, 2)),
                  pl.BlockSpec((None, taps, wb), lambda i: (layer, 0, 0)), vec, vec, vec],
        out_specs=pl.BlockSpec((tl, wb), lambda i: (i, 0)),
        out_shape=jax.ShapeDtypeStruct((t, wb), BF16),
        scratch_shapes=[pltpu.VMEM((tl + 2 * HALO, wb), F32), pltpu.VMEM((tl, wb), F32)],
        compiler_params=_params("arbitrary"),
        name="branch_a_conformer",
    )(p, p, p, p, p, p, p, conv_w, conv_b, ln_g, ln_b)


def _gmlp_kernel(u_ref, v_ref, gate_ref, g_ref, b_ref, ws_ref, bias_ref, o_ref, *, tl):
    nh, chunk = ws_ref.shape[0], ws_ref.shape[1]
    hd = o_ref.shape[1] // nh
    vn = _ln(v_ref[...].astype(F32), g_ref[...], b_ref[...]).astype(BF16)
    for c in range(tl // chunk):
        rows = slice(c * chunk, (c + 1) * chunk)
        for h in range(nh):
            cols = slice(h * hd, (h + 1) * hd)
            s = jnp.dot(ws_ref[h], vn[rows, cols], preferred_element_type=F32) + bias_ref[:, cols]
            y = u_ref[rows, cols].astype(F32) * s * _silu(gate_ref[rows, cols].astype(F32))
            o_ref[rows, cols] = y.astype(o_ref.dtype)


def _branch_b(p, seq_len, ln_g, ln_b, ws, bias_full, layer):
    t = p.shape[0]
    wb = p.shape[1] // N_IN_SLICES
    nh, chunk = ws.shape[1], ws.shape[2]
    tl = _tile(seq_len, SEQ_TILE)
    vec = _vec_spec(layer, wb)
    return pl.pallas_call(
        functools.partial(_gmlp_kernel, tl=tl),
        grid=(t // tl,),
        in_specs=[pl.BlockSpec((tl, wb), lambda i: (i, 3)), pl.BlockSpec((tl, wb), lambda i: (i, 4)),
                  pl.BlockSpec((tl, wb), lambda i: (i, 5)), vec, vec,
                  pl.BlockSpec((None, nh, chunk, chunk), lambda i: (layer, 0, 0, 0)),
                  pl.BlockSpec((None, chunk, wb), lambda i: (layer, 0, 0))],
        out_specs=pl.BlockSpec((tl, wb), lambda i: (i, 0)),
        out_shape=jax.ShapeDtypeStruct((t, wb), BF16),
        compiler_params=_params("arbitrary"),
        name="branch_b_gmlp",
    )(p, p, p, ln_g, ln_b, ws, bias_full)


def _short_conv_kernel(cc, cc_p, cc_n, cx, cx_p, cx_n, cb_ref, gate_ref, cw_ref, o_ref, z_ref, *, tl, tiles_per_seq):
    _fill_halo(z_ref, lambda a, b: a * b, (cc, cx), (cc_p, cx_p), (cc_n, cx_n), tl, tiles_per_seq)
    wb = o_ref.shape[1]
    rc = min(CONV_ROWS, tl)

    def chunk(ci, carry):
        r0 = pl.multiple_of(ci * rc, rc)
        for blk in range(wb // LANES):
            lanes = slice(blk * LANES, (blk + 1) * LANES)
            y = cb_ref[pl.ds(r0, rc), lanes].astype(F32) * _conv_block(z_ref, cw_ref, r0, rc, lanes)
            o_ref[pl.ds(r0, rc), lanes] = (y * _silu(gate_ref[pl.ds(r0, rc), lanes].astype(F32))).astype(o_ref.dtype)
        return carry

    lax.fori_loop(0, tl // rc, chunk, 0)


def _branch_c(p, seq_len, conv_w, layer):
    t = p.shape[0]
    wb = p.shape[1] // N_IN_SLICES
    tl = _tile(seq_len, SEQ_TILE)
    taps = conv_w.shape[1]
    return pl.pallas_call(
        functools.partial(_short_conv_kernel, tl=tl, tiles_per_seq=seq_len // tl),
        grid=(t // tl,),
        in_specs=[*_seq_specs(tl, wb, t, 7), *_seq_specs(tl, wb, t, 8),
                  pl.BlockSpec((tl, wb), lambda i: (i, 6)), pl.BlockSpec((tl, wb), lambda i: (i, 9)),
                  pl.BlockSpec((None, taps, wb), lambda i: (layer, 0, 0))],
        out_specs=pl.BlockSpec((tl, wb), lambda i: (i, 0)),
        out_shape=jax.ShapeDtypeStruct((t, wb), BF16),
        scratch_shapes=[pltpu.VMEM((tl + 2 * HALO, wb), F32)],
        compiler_params=_params("arbitrary"),
        name="branch_c_short_conv",
    )(p, p, p, p, p, p, p, p, conv_w)


_NT = (((1,), (1,)), ((), ()))


def _ctx_attn_kernel(q_ref, k_ref, v_ref, gate_ref, o_ref, *, nh):
    hd = o_ref.shape[1] // nh
    scale = hd ** -0.5
    for h in range(nh):
        cols = slice(h * hd, (h + 1) * hd)
        q = q_ref[:, cols].astype(BF16)
        s = lax.dot_general(q, k_ref[:, cols].astype(BF16), _NT, preferred_element_type=F32) * scale
        e = jnp.exp(s - jnp.max(s, axis=-1, keepdims=True))
        o = jnp.dot(e.astype(BF16), v_ref[:, cols].astype(BF16), preferred_element_type=F32)
        o = o / jnp.sum(e, axis=-1, keepdims=True)
        o_ref[:, cols] = (o * _silu(gate_ref[:, cols].astype(F32))).astype(o_ref.dtype)


def _branch_d_context(p, seq_len, nh):
    t = p.shape[0]
    wb = p.shape[1] // N_IN_SLICES
    return pl.pallas_call(
        functools.partial(_ctx_attn_kernel, nh=nh),
        grid=(t // seq_len,),
        in_specs=[pl.BlockSpec((seq_len, wb), lambda b, s=s: (b, s)) for s in (10, 11, 12, 13)],
        out_specs=pl.BlockSpec((seq_len, wb), lambda b: (b, 0)),
        out_shape=jax.ShapeDtypeStruct((t, wb), BF16),
        compiler_params=_params("arbitrary"),
        name="branch_d_context_attention",
    )(p, p, p, p)


def _nbr_attn_kernel(q_ref, k_ref, v_ref, gate_ref, ck_ref, cv_ref, bias_ref, o_ref, oc_ref, mc_ref, lc_ref,
                     *, rows, kr):
    hd = o_ref.shape[1]
    seq_len = q_ref.shape[0]
    win_r = bias_ref.shape[0] // 2 + 1
    scale = hd ** -0.5
    ck = ck_ref[...].astype(BF16)
    cv = cv_ref[...].astype(BF16)

    cq = _tile(seq_len, 512)

    def ctx_chunk(ci, carry):
        r0 = pl.multiple_of(ci * cq, cq)
        s = lax.dot_general(q_ref[pl.ds(r0, cq), :].astype(BF16), ck, _NT, preferred_element_type=F32) * scale
        m = jnp.max(s, axis=-1, keepdims=True)
        e = jnp.exp(s - m)
        mc_ref[pl.ds(r0, cq), :] = m
        lc_ref[pl.ds(r0, cq), :] = jnp.sum(e, axis=-1, keepdims=True)
        oc_ref[pl.ds(r0, cq), :] = jnp.dot(e.astype(BF16), cv, preferred_element_type=F32)
        return carry

    lax.fori_loop(0, seq_len // cq, ctx_chunk, 0, unroll=2 if (seq_len // cq) % 2 == 0 else 1)

    group = ATTN_ROW_GROUP if rows % ATTN_ROW_GROUP == 0 else 1

    def window_starts(gi):
        rs = [gi * group + i for i in range(group)]
        starts = [jnp.clip(r - kr // 2, 0, rows - kr) for r in rs]
        return rs, starts, [pl.multiple_of(s * GRID_W, GRID_W) for s in starts]

    def scores_stage(gi):
        rs, starts, k0s = window_starts(gi)
        out = []
        for r, s, k0 in zip(rs, starts, k0s):
            q0 = pl.multiple_of(r * GRID_W, GRID_W)
            base = win_r - 1 - (r - s)
            bias = jnp.concatenate([bias_ref[base + 2 * t] for t in range(kr // 2)], axis=1)
            out.append(lax.dot_general(q_ref[pl.ds(q0, GRID_W), :].astype(BF16),
                                       k_ref[pl.ds(k0, kr * GRID_W), :].astype(BF16), _NT,
                                       preferred_element_type=F32) * scale + bias)
        return tuple(out)

    def softmax_stage(scores):
        out = []
        for sc in scores:
            m = jnp.max(sc, axis=-1, keepdims=True)
            e = jnp.exp(sc - m)
            out.append((e.astype(BF16), m, jnp.sum(e, axis=-1, keepdims=True)))
        return tuple(out)

    def values_stage(gi, stats):
        rs, _, k0s = window_starts(gi)
        for r, k0, (e, m_l, l_l) in zip(rs, k0s, stats):
            q0 = pl.multiple_of(r * GRID_W, GRID_W)
            o_l = jnp.dot(e, v_ref[pl.ds(k0, kr * GRID_W), :].astype(BF16), preferred_element_type=F32)
            m_c = mc_ref[pl.ds(q0, GRID_W), :]
            m = jnp.maximum(m_l, m_c)
            a_l = jnp.exp(m_l - m)
            a_c = jnp.exp(m_c - m)
            o = (a_l * o_l + a_c * oc_ref[pl.ds(q0, GRID_W), :]) / (a_l * l_l + a_c * lc_ref[pl.ds(q0, GRID_W), :])
            g = gate_ref[pl.ds(q0, GRID_W), :].astype(F32)
            o_ref[pl.ds(q0, GRID_W), :] = (o * _silu(g)).astype(o_ref.dtype)

    def pipelined(gi, stats):
        nxt = softmax_stage(scores_stage(gi + 1))
        values_stage(gi, stats)
        return nxt

    n_groups = rows // group
    last = lax.fori_loop(0, n_groups - 1, pipelined, softmax_stage(scores_stage(jnp.int32(0))))
    values_stage(jnp.int32(n_groups - 1), last)


def _branch_d_latent(p, seq_len, cache_k, cache_v, bias_tab, layer):
    t = p.shape[0]
    wb = p.shape[1] // N_IN_SLICES
    nh, n_off = bias_tab.shape[1], bias_tab.shape[2]
    hd = wb // nh
    past = cache_k.shape[2]
    rows = seq_len // GRID_W
    kr = min(n_off // 2 + 1, rows)
    assert kr % 2 == 0
    col = lambda s: pl.BlockSpec((seq_len, hd), lambda b, h, s=s: (b, s * nh + h))
    ctx = pl.BlockSpec((None, None, past, hd), lambda b, h: (b, layer, 0, h))
    return pl.pallas_call(
        functools.partial(_nbr_attn_kernel, rows=rows, kr=kr),
        grid=(t // seq_len, nh),
        in_specs=[col(10), col(11), col(12), col(13), ctx, ctx,
                  pl.BlockSpec((None, None, n_off, GRID_W, 2 * GRID_W), lambda b, h: (layer, h, 0, 0, 0))],
        out_specs=pl.BlockSpec((seq_len, hd), lambda b, h: (b, h)),
        out_shape=jax.ShapeDtypeStruct((t, wb), BF16),
        scratch_shapes=[pltpu.VMEM((seq_len, hd), F32), pltpu.VMEM((seq_len, 1), F32), pltpu.VMEM((seq_len, 1), F32)],
        compiler_params=_params("arbitrary", "arbitrary"),
        name="branch_d_neighbourhood_attention",
    )(p, p, p, p, cache_k, cache_v, bias_tab)


def _neighbourhood_bias(rpb):
    win_c = (rpb.shape[3] + 1) // 2
    qc = np.arange(GRID_W)[:, None]
    kc = np.arange(GRID_W)[None, :]
    sc = np.clip(qc - win_c // 2, 0, GRID_W - win_c)
    valid = (kc >= sc) & (kc < sc + win_c)
    select = (kc - qc + win_c - 1 == np.arange(2 * win_c - 1)[:, None, None]) & valid
    cols = jnp.einsum("lhrd,dqk->lhrqk", rpb.astype(F32), jnp.asarray(select, F32), precision=lax.Precision.HIGHEST)
    cols = jnp.where(valid, cols, NEG)
    return jnp.concatenate([cols[:, :, :-1], cols[:, :, 1:]], axis=-1)


def kernel(x_prompt, x_sample, cache_k, cache_v, c, c_ctx, w_ada, b_ada, g_pre, g_post, w_in, a_conv_w, a_conv_b,
           a_ln_g, a_ln_b, b_ln_g, b_ln_b, b_ws, b_bias, c_conv_w, d_rpb, w_out):
    batch, seq, d = x_prompt.shape
    dec_batch, dec_seq, _ = x_sample.shape
    depth = w_in.shape[0]
    wb = d // 4
    nh, hd = cache_k.shape[3], cache_k.shape[4]
    past = cache_k.shape[2]
    assert nh * hd == wb and w_in.shape[2] == N_IN_SLICES * wb and dec_seq % GRID_W == 0

    n_rows = -(-(1 + dec_batch) // 8) * 8
    cond = jnp.zeros((n_rows, d), F32).at[0].set(c_ctx).at[1:1 + dec_batch].set(c)
    mod = _ada_modulation(cond, w_ada, b_ada).reshape(depth, n_rows, 3, 1, d)

    vec = lambda a: a.reshape(depth, 1, a.shape[-1])
    g_pre, g_post, a_conv_b, a_ln_g, a_ln_b, b_ln_g, b_ln_b = map(
        vec, (g_pre, g_post, a_conv_b, a_ln_g, a_ln_b, b_ln_g, b_ln_b))
    ws_b = b_ws.astype(BF16)
    gh = wb // b_ws.shape[1]
    bias_full = jnp.repeat(jnp.swapaxes(b_bias, 1, 2), gh, axis=2)
    bias_tab = _neighbourhood_bias(d_rpb)
    ck = cache_k.reshape(dec_batch, depth, past, wb)
    cv = cache_v.reshape(dec_batch, depth, past, wb)

    xs = [x_prompt.reshape(batch * seq, d), x_sample.reshape(dec_batch * dec_seq, d)]
    seqs = [seq, dec_seq]
    row0 = [0, 1]
    mod_seq = [batch * seq, dec_seq]
    p_dtype = [F32, BF16]
    hn = [_prenorm(xs[i], g_pre, mod, 0, row0[i], mod_seq[i]) for i in range(2)]
    new_k, new_v = [], []
    w_in_l, w_out_l = w_in[0].astype(BF16), w_out[0].astype(BF16)
    for l in range(depth):
        rounds_next = l + 1 < depth
        for i in range(2):
            if i == 1 and rounds_next:
                p, w_in_next = _in_proj(hn[i], w_in_l, p_dtype[i], w_in, l + 1)
            else:
                p = _in_proj(hn[i], w_in_l, p_dtype[i])
            ya = _branch_a(p, seqs[i], a_conv_w, a_conv_b, a_ln_g, a_ln_b, l)
            yb = _branch_b(p, seqs[i], b_ln_g, b_ln_b, ws_b, bias_full, l)
            yc = _branch_c(p, seqs[i], c_conv_w, l)
            if i == 0:
                yd = _branch_d_context(p, seq, nh)
                new_k.append(p[:, 11 * wb:12 * wb].reshape(batch, seq, nh, hd))
                new_v.append(p[:, 12 * wb:13 * wb].reshape(batch, seq, nh, hd))
            else:
                yd = _branch_d_latent(p, dec_seq, ck, cv, bias_tab, l)
            if i == 1 and rounds_next:
                y, w_out_next = _out_proj((ya, yb, yc, yd), w_out_l, w_out, l + 1)
            else:
                y = _out_proj((ya, yb, yc, yd), w_out_l)
            xs[i], hn[i] = _post(xs[i], y, g_post, g_pre, mod, l, row0[i], mod_seq[i], rounds_next)
        if rounds_next:
            w_in_l, w_out_l = w_in_next, w_out_next
    return (xs[0].reshape(batch, seq, d), xs[1].reshape(dec_batch, dec_seq, d),
            jnp.stack(new_k, axis=1), jnp.stack(new_v, axis=1))
```

```python
import collections
import functools

import jax
import jax.numpy as jnp
import numpy as np
from jax import lax
from jax.experimental import pallas as pl
from jax.experimental.pallas import tpu as pltpu

GRID_W = 64
EPS = 1e-6
NEG = -1e30
N_IN_SLICES = 14
HALO = 16
VMEM_LIMIT_BYTES = 56 * 1024 * 1024
MM_TILE = 1024
SEQ_TILE = 512
CONV_ROWS = 64
ATTN_ROW_GROUP = 4
LANES = 128
SUBLANES = 8

F32 = jnp.float32
BF16 = jnp.bfloat16


def _tile(n, target, unit=128):
    if n <= target:
        return n
    t = target - target % unit
    while n % t:
        t -= unit
    return t


def _params(*sem):
    return pltpu.CompilerParams(dimension_semantics=sem, vmem_limit_bytes=VMEM_LIMIT_BYTES)


def _silu(x):
    return x * jax.nn.sigmoid(x)


def _rms(x, g):
    return x * lax.rsqrt(jnp.mean(x * x, axis=-1, keepdims=True) + EPS) * g


def _ln(x, g, b):
    mu = jnp.mean(x, axis=-1, keepdims=True)
    xc = x - mu
    var = jnp.mean(xc * xc, axis=-1, keepdims=True)
    return xc * lax.rsqrt(var + EPS) * g + b


def _ada_kernel(cond_ref, w_ref, b_ref, o_ref):
    a = _silu(cond_ref[...]).astype(BF16)
    o_ref[...] = jnp.dot(a, w_ref[...].astype(BF16), preferred_element_type=F32) + b_ref[...]


def _ada_modulation(cond, w_ada, b_ada):
    depth, d, n = w_ada.shape
    r = cond.shape[0]
    tn = _tile(n, 512)
    return pl.pallas_call(
        _ada_kernel,
        grid=(depth, n // tn),
        in_specs=[pl.BlockSpec((r, d), lambda l, j: (0, 0)),
                  pl.BlockSpec((None, d, tn), lambda l, j: (l, 0, j)),
                  pl.BlockSpec((None, 1, tn), lambda l, j: (l, 0, j))],
        out_specs=pl.BlockSpec((None, r, tn), lambda l, j: (l, 0, j)),
        out_shape=jax.ShapeDtypeStruct((depth, r, n), F32),
        compiler_params=_params("arbitrary", "arbitrary"),
        name="ada_modulation",
    )(cond, w_ada, b_ada.reshape(depth, 1, n))


def _prenorm_kernel(x_ref, g_ref, mod_ref, o_ref):
    y = _rms(x_ref[...], g_ref[...])
    o_ref[...] = (y * (1.0 + mod_ref[1]) + mod_ref[0]).astype(o_ref.dtype)


def _post_kernel(x_ref, y_ref, gpost_ref, mod_ref, *rest, has_next):
    xn = x_ref[...] + mod_ref[2] * _rms(y_ref[...].astype(F32), gpost_ref[...])
    if has_next:
        gpre_ref, modn_ref, xo_ref, hn_ref = rest
        hn_ref[...] = (_rms(xn, gpre_ref[...]) * (1.0 + modn_ref[1]) + modn_ref[0]).astype(hn_ref.dtype)
    else:
        (xo_ref,) = rest
    xo_ref[...] = xn


def _row_tile(t):
    return min(256, t)


def _vec_spec(layer, width):
    return pl.BlockSpec((None, 1, width), lambda *_: (layer, 0, 0))


def _mod_spec(layer, row0, tiles_per_row, d, step=lambda i: i):
    return pl.BlockSpec((None, None, 3, 1, d), lambda *g: (layer, row0 + step(*g) // tiles_per_row, 0, 0, 0))


Rider = collections.namedtuple("Rider", "fn in_specs args out_specs out_shape")


def _run_riders(riders, in_refs, out_refs):
    for r in riders:
        n_in, n_out = len(r.in_specs), len(r.out_specs)
        r.fn(*in_refs[:n_in], *out_refs[:n_out])
        in_refs, out_refs = in_refs[n_in:], out_refs[n_out:]


def _prenorm(x, g, mod, layer, row0, seq_len):
    t, d = x.shape
    tm = _row_tile(seq_len)
    return pl.pallas_call(
        _prenorm_kernel,
        grid=(t // tm,),
        in_specs=[pl.BlockSpec((tm, d), lambda i: (i, 0)),
                  _vec_spec(layer, d),
                  _mod_spec(layer, row0, seq_len // tm, d)],
        out_specs=pl.BlockSpec((tm, d), lambda i: (i, 0)),
        out_shape=jax.ShapeDtypeStruct((t, d), BF16),
        compiler_params=_params("arbitrary"),
        name="prenorm",
    )(x, g, mod)


def _post_rider(x, y, g_post, g_pre, mod, layer, row0, seq_len, has_next, tm, step):
    t, d = x.shape
    tpr = seq_len // tm
    row = pl.BlockSpec((tm, d), lambda *g: (step(*g), 0))
    in_specs = [row, row, _vec_spec(layer, d), _mod_spec(layer, row0, tpr, d, step)]
    args = [x, y, g_post, mod]
    out_specs = [row]
    out_shape = [jax.ShapeDtypeStruct((t, d), F32)]
    if has_next:
        in_specs += [_vec_spec(layer + 1, d), _mod_spec(layer + 1, row0, tpr, d, step)]
        args += [g_pre, mod]
        out_specs.append(row)
        out_shape.append(jax.ShapeDtypeStruct((t, d), BF16))
    return Rider(functools.partial(_post_kernel, has_next=has_next), in_specs, args, out_specs, out_shape)


def _post(x, y, g_post, g_pre, mod, layer, row0, seq_len, has_next):
    tm = _row_tile(seq_len)
    r = _post_rider(x, y, g_post, g_pre, mod, layer, row0, seq_len, has_next, tm, lambda i: i)
    out = pl.pallas_call(
        r.fn,
        grid=(x.shape[0] // tm,),
        in_specs=r.in_specs,
        out_specs=r.out_specs,
        out_shape=r.out_shape,
        compiler_params=_params("arbitrary"),
        name="post",
    )(*r.args)
    return (out[0], out[1]) if has_next else (out[0], None)


def _round_kernel(w_ref, o_ref):
    o_ref[...] = w_ref[...].astype(o_ref.dtype)


def _weight_round_rider(w_all, layer, grid_rows, tn):
    k = w_all.shape[1]
    rows = k // grid_rows
    assert rows * grid_rows == k and rows % (2 * SUBLANES) == 0
    return Rider(_round_kernel,
                 [pl.BlockSpec((None, rows, tn), lambda i, j: (layer, i, j))], [w_all],
                 [pl.BlockSpec((rows, tn), lambda i, j: (i, j))], [jax.ShapeDtypeStruct(w_all.shape[1:], BF16)])


def _rider_lists(riders):
    cat = lambda field: [v for r in riders for v in getattr(r, field)]
    return cat("in_specs"), cat("args"), cat("out_specs"), cat("out_shape")


def _in_proj_kernel(x_ref, w_ref, *rest, riders):
    n_in = sum(len(r.in_specs) for r in riders)
    o_ref = rest[n_in]
    o_ref[...] = jnp.dot(x_ref[...], w_ref[...], preferred_element_type=F32).astype(o_ref.dtype)
    _run_riders(riders, rest[:n_in], rest[n_in + 1:])


def _proj_tiles(t, n):
    return _tile(t, MM_TILE), _tile(n, MM_TILE)


def _out_proj_tiles(t, n):
    return _tile(t, MM_TILE), _tile(n, MM_TILE // 2)


def _in_proj(hn, w, out_dtype, riders=()):
    t, d = hn.shape
    n = w.shape[1]
    tm, tn = _proj_tiles(t, n)
    r_in, r_args, r_out, r_shape = _rider_lists(riders)
    return pl.pallas_call(
        functools.partial(_in_proj_kernel, riders=riders),
        grid=(t // tm, n // tn),
        in_specs=[pl.BlockSpec((tm, d), lambda i, j: (i, 0)), pl.BlockSpec((d, tn), lambda i, j: (0, j)), *r_in],
        out_specs=[pl.BlockSpec((tm, tn), lambda i, j: (i, j)), *r_out],
        out_shape=[jax.ShapeDtypeStruct((t, n), out_dtype), *r_shape],
        compiler_params=_params("arbitrary", "arbitrary"),
        name="in_proj",
    )(hn, w, *r_args)


def _out_proj_kernel(a_ref, b_ref, c_ref, d_ref, w_ref, *rest, riders):
    n_in = sum(len(r.in_specs) for r in riders)
    o_ref, cat_ref = rest[n_in], rest[-1]
    wb = a_ref.shape[1]

    @pl.when(pl.program_id(1) == 0)
    def _():
        for s, ref in enumerate((a_ref, b_ref, c_ref, d_ref)):
            cat_ref[:, s * wb:(s + 1) * wb] = ref[...]

    o_ref[...] = jnp.dot(cat_ref[...], w_ref[...], preferred_element_type=F32).astype(o_ref.dtype)
    _run_riders(riders, rest[:n_in], rest[n_in + 1:-1])


def _out_proj(branches, w, riders=()):
    t, wb = branches[0].shape
    k, n = w.shape
    tm, tn = _out_proj_tiles(t, n)
    slab = pl.BlockSpec((tm, wb), lambda i, j: (i, 0))
    r_in, r_args, r_out, r_shape = _rider_lists(riders)
    return pl.pallas_call(
        functools.partial(_out_proj_kernel, riders=riders),
        grid=(t // tm, n // tn),
        in_specs=[slab, slab, slab, slab, pl.BlockSpec((k, tn), lambda i, j: (0, j)), *r_in],
        out_specs=[pl.BlockSpec((tm, tn), lambda i, j: (i, j)), *r_out],
        out_shape=[jax.ShapeDtypeStruct((t, n), BF16), *r_shape],
        scratch_shapes=[pltpu.VMEM((tm, k), BF16)],
        compiler_params=_params("arbitrary", "arbitrary"),
        name="out_proj",
    )(*branches, w, *r_args)


def _seq_specs(tl, wb, total_rows, cols):
    r = tl // HALO
    last = total_rows // HALO - 1
    main = pl.BlockSpec((tl, wb), lambda i: (i, cols))
    prev = pl.BlockSpec((HALO, wb), lambda i: (jnp.maximum(i * r - 1, 0), cols))
    nxt = pl.BlockSpec((HALO, wb), lambda i: (jnp.minimum((i + 1) * r, last), cols))
    return main, prev, nxt


def _fill_halo(z_ref, fn, main, prev, nxt, tl, tiles_per_seq):
    i = pl.program_id(0) % tiles_per_seq
    z_ref[0:HALO, :] = jnp.where(i != 0, fn(*[r[...].astype(F32) for r in prev]), 0.0)
    z_ref[HALO:HALO + tl, :] = fn(*[r[...].astype(F32) for r in main])
    z_ref[HALO + tl:, :] = jnp.where(i != tiles_per_seq - 1, fn(*[r[...].astype(F32) for r in nxt]), 0.0)


def _conv_block(z_ref, cw_ref, r0, rc, lanes):
    taps = cw_ref.shape[0]
    first = HALO - (taps - 1) // 2
    win = z_ref[pl.ds(r0, rc + 2 * HALO), lanes]
    acc = None
    for s in range(SUBLANES):
        offs = [o for o in range(first, first + taps) if o % SUBLANES == s]
        if offs:
            rolled = pltpu.roll(win, win.shape[0] - s, 0) if s else win
            for o in offs:
                term = cw_ref[o - first:o - first + 1, lanes] * rolled[o - s:o - s + rc]
                acc = term if acc is None else acc + term
    return acc


def _conformer_kernel(val, val_p, val_n, glu, glu_p, glu_n, gate_ref, cw_ref, cb_ref, g_ref, b_ref, o_ref, z_ref,
                      y_ref, *, tl, tiles_per_seq):
    _fill_halo(z_ref, lambda v, g: v * jax.nn.sigmoid(g), (val, glu), (val_p, glu_p), (val_n, glu_n), tl, tiles_per_seq)
    wb = o_ref.shape[1]
    rc = min(CONV_ROWS, tl)

    def chunk(ci, carry):
        r0 = pl.multiple_of(ci * rc, rc)
        for cb in range(wb // LANES):
            lanes = slice(cb * LANES, (cb + 1) * LANES)
            y_ref[pl.ds(r0, rc), lanes] = _conv_block(z_ref, cw_ref, r0, rc, lanes) + cb_ref[:, lanes]
        y = _silu(_ln(y_ref[pl.ds(r0, rc), :], g_ref[...], b_ref[...]))
        o_ref[pl.ds(r0, rc), :] = (y * _silu(gate_ref[pl.ds(r0, rc), :].astype(F32))).astype(o_ref.dtype)
        return carry

    lax.fori_loop(0, tl // rc, chunk, 0)


def _branch_a(p, seq_len, conv_w, conv_b, ln_g, ln_b, layer):
    t = p.shape[0]
    wb = p.shape[1] // N_IN_SLICES
    tl = _tile(seq_len, SEQ_TILE)
    taps = conv_w.shape[1]
    v_specs = _seq_specs(tl, wb, t, 0)
    g_specs = _seq_specs(tl, wb, t, 1)
    vec = _vec_spec(layer, wb)
    return pl.pallas_call(
        functools.partial(_conformer_kernel, tl=tl, tiles_per_seq=seq_len // tl),
        grid=(t // tl,),
        in_specs=[*v_specs, *g_specs, pl.BlockSpec((tl, wb), lambda i: (i, 2)),
                  pl.BlockSpec((None, taps, wb), lambda i: (layer, 0, 0)), vec, vec, vec],
        out_specs=pl.BlockSpec((tl, wb), lambda i: (i, 0)),
        out_shape=jax.ShapeDtypeStruct((t, wb), BF16),
        scratch_shapes=[pltpu.VMEM((tl + 2 * HALO, wb), F32), pltpu.VMEM((tl, wb), F32)],
        compiler_params=_params("arbitrary"),
        name="branch_a_conformer",
    )(p, p, p, p, p, p, p, conv_w, conv_b, ln_g, ln_b)


def _gmlp_kernel(u_ref, v_ref, gate_ref, g_ref, b_ref, ws_ref, bias_ref, o_ref, *, tl):
    nh, chunk = ws_ref.shape[0], ws_ref.shape[1]
    hd = o_ref.shape[1] // nh
    vn = _ln(v_ref[...].astype(F32), g_ref[...], b_ref[...]).astype(BF16)
    for c in range(tl // chunk):
        rows = slice(c * chunk, (c + 1) * chunk)
        for h in range(nh):
            cols = slice(h * hd, (h + 1) * hd)
            s = jnp.dot(ws_ref[h], vn[rows, cols], preferred_element_type=F32) + bias_ref[:, cols]
            y = u_ref[rows, cols].astype(F32) * s * _silu(gate_ref[rows, cols].astype(F32))
            o_ref[rows, cols] = y.astype(o_ref.dtype)


def _branch_b(p, seq_len, ln_g, ln_b, ws, bias_full, layer):
    t = p.shape[0]
    wb = p.shape[1] // N_IN_SLICES
    nh, chunk = ws.shape[1], ws.shape[2]
    tl = _tile(seq_len, SEQ_TILE)
    vec = _vec_spec(layer, wb)
    return pl.pallas_call(
        functools.partial(_gmlp_kernel, tl=tl),
        grid=(t // tl,),
        in_specs=[pl.BlockSpec((tl, wb), lambda i: (i, 3)), pl.BlockSpec((tl, wb), lambda i: (i, 4)),
                  pl.BlockSpec((tl, wb), lambda i: (i, 5)), vec, vec,
                  pl.BlockSpec((None, nh, chunk, chunk), lambda i: (layer, 0, 0, 0)),
                  pl.BlockSpec((None, chunk, wb), lambda i: (layer, 0, 0))],
        out_specs=pl.BlockSpec((tl, wb), lambda i: (i, 0)),
        out_shape=jax.ShapeDtypeStruct((t, wb), BF16),
        compiler_params=_params("arbitrary"),
        name="branch_b_gmlp",
    )(p, p, p, ln_g, ln_b, ws, bias_full)


def _short_conv_kernel(cc, cc_p, cc_n, cx, cx_p, cx_n, cb_ref, gate_ref, cw_ref, o_ref, z_ref, *, tl, tiles_per_seq):
    _fill_halo(z_ref, lambda a, b: a * b, (cc, cx), (cc_p, cx_p), (cc_n, cx_n), tl, tiles_per_seq)
    wb = o_ref.shape[1]
    rc = min(CONV_ROWS, tl)

    def chunk(ci, carry):
        r0 = pl.multiple_of(ci * rc, rc)
        for blk in range(wb // LANES):
            lanes = slice(blk * LANES, (blk + 1) * LANES)
            y = cb_ref[pl.ds(r0, rc), lanes].astype(F32) * _conv_block(z_ref, cw_ref, r0, rc, lanes)
            o_ref[pl.ds(r0, rc), lanes] = (y * _silu(gate_ref[pl.ds(r0, rc), lanes].astype(F32))).astype(o_ref.dtype)
        return carry

    lax.fori_loop(0, tl // rc, chunk, 0)


def _branch_c(p, seq_len, conv_w, layer):
    t = p.shape[0]
    wb = p.shape[1] // N_IN_SLICES
    tl = _tile(seq_len, SEQ_TILE)
    taps = conv_w.shape[1]
    return pl.pallas_call(
        functools.partial(_short_conv_kernel, tl=tl, tiles_per_seq=seq_len // tl),
        grid=(t // tl,),
        in_specs=[*_seq_specs(tl, wb, t, 7), *_seq_specs(tl, wb, t, 8),
                  pl.BlockSpec((tl, wb), lambda i: (i, 6)), pl.BlockSpec((tl, wb), lambda i: (i, 9)),
                  pl.BlockSpec((None, taps, wb), lambda i: (layer, 0, 0))],
        out_specs=pl.BlockSpec((tl, wb), lambda i: (i, 0)),
        out_shape=jax.ShapeDtypeStruct((t, wb), BF16),
        scratch_shapes=[pltpu.VMEM((tl + 2 * HALO, wb), F32)],
        compiler_params=_params("arbitrary"),
        name="branch_c_short_conv",
    )(p, p, p, p, p, p, p, p, conv_w)


_NT = (((1,), (1,)), ((), ()))


def _ctx_attn_kernel(q_ref, k_ref, v_ref, gate_ref, kall_ref, vall_ref, o_ref, knew_ref, vnew_ref, *, nh):
    del kall_ref, vall_ref
    knew_ref[...] = k_ref[...].astype(knew_ref.dtype)
    vnew_ref[...] = v_ref[...].astype(vnew_ref.dtype)
    hd = o_ref.shape[1] // nh
    scale = hd ** -0.5
    for h in range(nh):
        cols = slice(h * hd, (h + 1) * hd)
        q = q_ref[:, cols].astype(BF16)
        s = lax.dot_general(q, k_ref[:, cols].astype(BF16), _NT, preferred_element_type=F32) * scale
        e = jnp.exp(s - jnp.max(s, axis=-1, keepdims=True))
        o = jnp.dot(e.astype(BF16), v_ref[:, cols].astype(BF16), preferred_element_type=F32)
        o = o / jnp.sum(e, axis=-1, keepdims=True)
        o_ref[:, cols] = (o * _silu(gate_ref[:, cols].astype(F32))).astype(o_ref.dtype)


def _branch_d_context(p, seq_len, nh, layer, new_k, new_v):
    t = p.shape[0]
    wb = p.shape[1] // N_IN_SLICES
    kv_out = pl.BlockSpec((None, None, seq_len, wb), lambda b: (b, layer, 0, 0))
    any_space = pl.BlockSpec(memory_space=pl.ANY)
    return pl.pallas_call(
        functools.partial(_ctx_attn_kernel, nh=nh),
        grid=(t // seq_len,),
        in_specs=[*[pl.BlockSpec((seq_len, wb), lambda b, s=s: (b, s)) for s in (10, 11, 12, 13)], any_space, any_space],
        out_specs=[pl.BlockSpec((seq_len, wb), lambda b: (b, 0)), kv_out, kv_out],
        out_shape=[jax.ShapeDtypeStruct((t, wb), BF16), jax.ShapeDtypeStruct(new_k.shape, new_k.dtype),
                   jax.ShapeDtypeStruct(new_v.shape, new_v.dtype)],
        input_output_aliases={4: 1, 5: 2},
        compiler_params=_params("arbitrary"),
        name="branch_d_context_attention",
    )(p, p, p, p, new_k, new_v)


def _nbr_attn_kernel(q_ref, k_ref, v_ref, gate_ref, ck_ref, cv_ref, bias_ref, o_ref, oc_ref, mc_ref, lc_ref,
                     *, rows, kr):
    hd = o_ref.shape[1]
    seq_len = q_ref.shape[0]
    win_r = bias_ref.shape[0] // 2 + 1
    scale = hd ** -0.5
    ck = ck_ref[...].astype(BF16)
    cv = cv_ref[...].astype(BF16)

    cq = _tile(seq_len, 512)

    def ctx_chunk(ci, carry):
        r0 = pl.multiple_of(ci * cq, cq)
        s = lax.dot_general(q_ref[pl.ds(r0, cq), :].astype(BF16), ck, _NT, preferred_element_type=F32) * scale
        m = jnp.max(s, axis=-1, keepdims=True)
        e = jnp.exp(s - m)
        mc_ref[pl.ds(r0, cq), :] = m
        lc_ref[pl.ds(r0, cq), :] = jnp.sum(e, axis=-1, keepdims=True)
        oc_ref[pl.ds(r0, cq), :] = jnp.dot(e.astype(BF16), cv, preferred_element_type=F32)
        return carry

    lax.fori_loop(0, seq_len // cq, ctx_chunk, 0, unroll=2 if (seq_len // cq) % 2 == 0 else 1)

    group = ATTN_ROW_GROUP if rows % ATTN_ROW_GROUP == 0 else 1

    def window_starts(gi):
        rs = [gi * group + i for i in range(group)]
        starts = [jnp.clip(r - kr // 2, 0, rows - kr) for r in rs]
        return rs, starts, [pl.multiple_of(s * GRID_W, GRID_W) for s in starts]

    def scores_stage(gi):
        rs, starts, k0s = window_starts(gi)
        out = []
        for r, s, k0 in zip(rs, starts, k0s):
            q0 = pl.multiple_of(r * GRID_W, GRID_W)
            base = win_r - 1 - (r - s)
            bias = jnp.concatenate([bias_ref[base + 2 * t] for t in range(kr // 2)], axis=1)
            out.append(lax.dot_general(q_ref[pl.ds(q0, GRID_W), :].astype(BF16),
                                       k_ref[pl.ds(k0, kr * GRID_W), :].astype(BF16), _NT,
                                       preferred_element_type=F32) * scale + bias)
        return tuple(out)

    def softmax_stage(scores):
        out = []
        for sc in scores:
            m = jnp.max(sc, axis=-1, keepdims=True)
            e = jnp.exp(sc - m)
            out.append((e.astype(BF16), m, jnp.sum(e, axis=-1, keepdims=True)))
        return tuple(out)

    def values_stage(gi, stats):
        rs, _, k0s = window_starts(gi)
        for r, k0, (e, m_l, l_l) in zip(rs, k0s, stats):
            q0 = pl.multiple_of(r * GRID_W, GRID_W)
            o_l = jnp.dot(e, v_ref[pl.ds(k0, kr * GRID_W), :].astype(BF16), preferred_element_type=F32)
            m_c = mc_ref[pl.ds(q0, GRID_W), :]
            m = jnp.maximum(m_l, m_c)
            a_l = jnp.exp(m_l - m)
            a_c = jnp.exp(m_c - m)
            o = (a_l * o_l + a_c * oc_ref[pl.ds(q0, GRID_W), :]) / (a_l * l_l + a_c * lc_ref[pl.ds(q0, GRID_W), :])
            g = gate_ref[pl.ds(q0, GRID_W), :].astype(F32)
            o_ref[pl.ds(q0, GRID_W), :] = (o * _silu(g)).astype(o_ref.dtype)

    def pipelined(gi, stats):
        nxt = softmax_stage(scores_stage(gi + 1))
        values_stage(gi, stats)
        return nxt

    n_groups = rows // group
    last = lax.fori_loop(0, n_groups - 1, pipelined, softmax_stage(scores_stage(jnp.int32(0))))
    values_stage(jnp.int32(n_groups - 1), last)


def _branch_d_latent(p, seq_len, cache_k, cache_v, bias_tab, layer):
    t = p.shape[0]
    wb = p.shape[1] // N_IN_SLICES
    nh, n_off = bias_tab.shape[1], bias_tab.shape[2]
    hd = wb // nh
    past = cache_k.shape[2]
    rows = seq_len // GRID_W
    kr = min(n_off // 2 + 1, rows)
    assert kr % 2 == 0
    col = lambda s: pl.BlockSpec((seq_len, hd), lambda b, h, s=s: (b, s * nh + h))
    ctx = pl.BlockSpec((None, None, past, hd), lambda b, h: (b, layer, 0, h))
    return pl.pallas_call(
        functools.partial(_nbr_attn_kernel, rows=rows, kr=kr),
        grid=(t // seq_len, nh),
        in_specs=[col(10), col(11), col(12), col(13), ctx, ctx,
                  pl.BlockSpec((None, None, n_off, GRID_W, 2 * GRID_W), lambda b, h: (layer, h, 0, 0, 0))],
        out_specs=pl.BlockSpec((seq_len, hd), lambda b, h: (b, h)),
        out_shape=jax.ShapeDtypeStruct((t, wb), BF16),
        scratch_shapes=[pltpu.VMEM((seq_len, hd), F32), pltpu.VMEM((seq_len, 1), F32), pltpu.VMEM((seq_len, 1), F32)],
        compiler_params=_params("arbitrary", "arbitrary"),
        name="branch_d_neighbourhood_attention",
    )(p, p, p, p, cache_k, cache_v, bias_tab)


def _neighbourhood_bias(rpb):
    win_c = (rpb.shape[3] + 1) // 2
    qc = np.arange(GRID_W)[:, None]
    kc = np.arange(GRID_W)[None, :]
    sc = np.clip(qc - win_c // 2, 0, GRID_W - win_c)
    valid = (kc >= sc) & (kc < sc + win_c)
    select = (kc - qc + win_c - 1 == np.arange(2 * win_c - 1)[:, None, None]) & valid
    cols = jnp.einsum("lhrd,dqk->lhrqk", rpb.astype(F32), jnp.asarray(select, F32), precision=lax.Precision.HIGHEST)
    cols = jnp.where(valid, cols, NEG)
    return jnp.concatenate([cols[:, :, :-1], cols[:, :, 1:]], axis=-1)


def kernel(x_prompt, x_sample, cache_k, cache_v, c, c_ctx, w_ada, b_ada, g_pre, g_post, w_in, a_conv_w, a_conv_b,
           a_ln_g, a_ln_b, b_ln_g, b_ln_b, b_ws, b_bias, c_conv_w, d_rpb, w_out):
    batch, seq, d = x_prompt.shape
    dec_batch, dec_seq, _ = x_sample.shape
    depth = w_in.shape[0]
    wb = d // 4
    nh, hd = cache_k.shape[3], cache_k.shape[4]
    past = cache_k.shape[2]
    assert nh * hd == wb and w_in.shape[2] == N_IN_SLICES * wb and dec_seq % GRID_W == 0

    n_rows = -(-(1 + dec_batch) // 8) * 8
    cond = jnp.zeros((n_rows, d), F32).at[0].set(c_ctx).at[1:1 + dec_batch].set(c)
    mod = _ada_modulation(cond, w_ada, b_ada).reshape(depth, n_rows, 3, 1, d)

    vec = lambda a: a.reshape(depth, 1, a.shape[-1])
    g_pre, g_post, a_conv_b, a_ln_g, a_ln_b, b_ln_g, b_ln_b = map(
        vec, (g_pre, g_post, a_conv_b, a_ln_g, a_ln_b, b_ln_g, b_ln_b))
    ws_b = b_ws.astype(BF16)
    gh = wb // b_ws.shape[1]
    bias_full = jnp.repeat(jnp.swapaxes(b_bias, 1, 2), gh, axis=2)
    bias_tab = _neighbourhood_bias(d_rpb)
    ck = cache_k.reshape(dec_batch, depth, past, wb)
    cv = cache_v.reshape(dec_batch, depth, past, wb)

    xs = [x_prompt.reshape(batch * seq, d), x_sample.reshape(dec_batch * dec_seq, d)]
    seqs = [seq, dec_seq]
    row0 = [0, 1]
    mod_seq = [batch * seq, dec_seq]
    p_dtype = [F32, BF16]
    hn = [_prenorm(xs[i], g_pre, mod, 0, row0[i], mod_seq[i]) for i in range(2)]
    new_k = jnp.zeros((batch, depth, seq, wb), F32)
    new_v = jnp.zeros((batch, depth, seq, wb), F32)
    w_in_l, w_out_l = w_in[0].astype(BF16), w_out[0].astype(BF16)
    lat_tiles = _out_proj_tiles(xs[1].shape[0], d)
    lat_grid = (xs[1].shape[0] // lat_tiles[0], d // lat_tiles[1])
    ctx_rows = xs[0].shape[0] // (lat_grid[0] * lat_grid[1])
    ctx_post_rides = ctx_rows * lat_grid[0] * lat_grid[1] == xs[0].shape[0] and ctx_rows % (2 * SUBLANES) == 0
    for l in range(depth):
        has_next = l + 1 < depth

        def branches(i, p):
            ya = _branch_a(p, seqs[i], a_conv_w, a_conv_b, a_ln_g, a_ln_b, l)
            yb = _branch_b(p, seqs[i], b_ln_g, b_ln_b, ws_b, bias_full, l)
            yc = _branch_c(p, seqs[i], c_conv_w, l)
            return ya, yb, yc

        p = _in_proj(hn[0], w_in_l, p_dtype[0])[0]
        yd, new_k, new_v = _branch_d_context(p, seq, nh, l, new_k, new_v)
        y_ctx = _out_proj((*branches(0, p), yd), w_out_l)[0]
        post_ctx = (xs[0], y_ctx, g_post, g_pre, mod, l, row0[0], mod_seq[0], has_next)
        if not ctx_post_rides:
            xs[0], hn[0] = _post(*post_ctx)

        in_riders = (_weight_round_rider(w_in, l + 1, lat_grid[0], _proj_tiles(xs[1].shape[0], w_in.shape[2])[1]),) \
            if has_next else ()
        p, *rounded = _in_proj(hn[1], w_in_l, p_dtype[1], in_riders)
        w_in_l = rounded[0] if has_next else None
        yd = _branch_d_latent(p, dec_seq, ck, cv, bias_tab, l)
        out_riders = (_weight_round_rider(w_out, l + 1, lat_grid[0], lat_tiles[1]),) if has_next else ()
        if ctx_post_rides:
            out_riders += (_post_rider(*post_ctx, ctx_rows, lambda i, j: i * lat_grid[1] + j),)
        y, *extra = _out_proj((*branches(1, p), yd), w_out_l, out_riders)
        if has_next:
            w_out_l, *extra = extra
        if ctx_post_rides:
            xs[0], hn[0] = (extra[0], extra[1]) if has_next else (extra[0], None)
        xs[1], hn[1] = _post(xs[1], y, g_post, g_pre, mod, l, row0[1], mod_seq[1], has_next)
    return (xs[0].reshape(batch, seq, d), xs[1].reshape(dec_batch, dec_seq, d),
            new_k.reshape(batch, depth, seq, nh, hd), new_v.reshape(batch, depth, seq, nh, hd))
```

```python
import collections
import functools

import jax
import jax.numpy as jnp
import numpy as np
from jax import lax
from jax.experimental import pallas as pl
from jax.experimental.pallas import tpu as pltpu

GRID_W = 64
EPS = 1e-6
NEG = -1e30
N_IN_SLICES = 14
HALO = 16
VMEM_LIMIT_BYTES = 56 * 1024 * 1024
MM_TILE = 1024
SEQ_TILE = 512
CONV_ROWS = 64
ATTN_HEADS_PER_STEP = 2
ATTN_ROW_GROUP = 4
LANES = 128
SUBLANES = 8

F32 = jnp.float32
BF16 = jnp.bfloat16


def _tile(n, target, unit=128):
    if n <= target:
        return n
    t = target - target % unit
    while n % t:
        t -= unit
    return t


def _params(*sem):
    return pltpu.CompilerParams(dimension_semantics=sem, vmem_limit_bytes=VMEM_LIMIT_BYTES)


def _silu(x):
    return x * jax.nn.sigmoid(x)


def _rms(x, g):
    return x * lax.rsqrt(jnp.mean(x * x, axis=-1, keepdims=True) + EPS) * g


def _ln(x, g, b):
    mu = jnp.mean(x, axis=-1, keepdims=True)
    xc = x - mu
    var = jnp.mean(xc * xc, axis=-1, keepdims=True)
    return xc * lax.rsqrt(var + EPS) * g + b


def _ada_kernel(cond_ref, w_ref, b_ref, o_ref):
    a = _silu(cond_ref[...]).astype(BF16)
    o_ref[...] = jnp.dot(a, w_ref[...].astype(BF16), preferred_element_type=F32) + b_ref[...]


def _ada_modulation(cond, w_ada, b_ada):
    depth, d, n = w_ada.shape
    r = cond.shape[0]
    tn = _tile(n, 512)
    return pl.pallas_call(
        _ada_kernel,
        grid=(depth, n // tn),
        in_specs=[pl.BlockSpec((r, d), lambda l, j: (0, 0)),
                  pl.BlockSpec((None, d, tn), lambda l, j: (l, 0, j)),
                  pl.BlockSpec((None, 1, tn), lambda l, j: (l, 0, j))],
        out_specs=pl.BlockSpec((None, r, tn), lambda l, j: (l, 0, j)),
        out_shape=jax.ShapeDtypeStruct((depth, r, n), F32),
        compiler_params=_params("arbitrary", "arbitrary"),
        name="ada_modulation",
    )(cond, w_ada, b_ada.reshape(depth, 1, n))


def _prenorm_kernel(x_ref, g_ref, mod_ref, o_ref):
    y = _rms(x_ref[...], g_ref[...])
    o_ref[...] = (y * (1.0 + mod_ref[1]) + mod_ref[0]).astype(o_ref.dtype)


def _post_kernel(x_ref, y_ref, gpost_ref, mod_ref, *rest, has_next):
    xn = x_ref[...] + mod_ref[2] * _rms(y_ref[...].astype(F32), gpost_ref[...])
    if has_next:
        gpre_ref, modn_ref, xo_ref, hn_ref = rest
        hn_ref[...] = (_rms(xn, gpre_ref[...]) * (1.0 + modn_ref[1]) + modn_ref[0]).astype(hn_ref.dtype)
    else:
        (xo_ref,) = rest
    xo_ref[...] = xn


def _row_tile(t):
    return min(256, t)


def _vec_spec(layer, width):
    return pl.BlockSpec((None, 1, width), lambda *_: (layer, 0, 0))


def _mod_spec(layer, row0, tiles_per_row, d):
    return pl.BlockSpec((None, None, 3, 1, d), lambda i: (layer, row0 + i // tiles_per_row, 0, 0, 0))


Rider = collections.namedtuple("Rider", "fn in_specs args out_specs out_shape")


def _run_riders(riders, in_refs, out_refs):
    for r in riders:
        n_in, n_out = len(r.in_specs), len(r.out_specs)
        r.fn(*in_refs[:n_in], *out_refs[:n_out])
        in_refs, out_refs = in_refs[n_in:], out_refs[n_out:]


def _prenorm(x, g, mod, layer, row0, seq_len):
    t, d = x.shape
    tm = _row_tile(seq_len)
    return pl.pallas_call(
        _prenorm_kernel,
        grid=(t // tm,),
        in_specs=[pl.BlockSpec((tm, d), lambda i: (i, 0)),
                  _vec_spec(layer, d),
                  _mod_spec(layer, row0, seq_len // tm, d)],
        out_specs=pl.BlockSpec((tm, d), lambda i: (i, 0)),
        out_shape=jax.ShapeDtypeStruct((t, d), BF16),
        compiler_params=_params("arbitrary"),
        name="prenorm",
    )(x, g, mod)


def _post(x, y, g_post, g_pre, mod, layer, row0, seq_len, has_next):
    t, d = x.shape
    tm = _row_tile(seq_len)
    tpr = seq_len // tm
    row = pl.BlockSpec((tm, d), lambda i: (i, 0))
    in_specs = [row, row, _vec_spec(layer, d), _mod_spec(layer, row0, tpr, d)]
    args = [x, y, g_post, mod]
    out_specs = [row]
    out_shape = [jax.ShapeDtypeStruct((t, d), F32)]
    if has_next:
        in_specs += [_vec_spec(layer + 1, d), _mod_spec(layer + 1, row0, tpr, d)]
        args += [g_pre, mod]
        out_specs.append(row)
        out_shape.append(jax.ShapeDtypeStruct((t, d), BF16))
    out = pl.pallas_call(
        functools.partial(_post_kernel, has_next=has_next),
        grid=(t // tm,),
        in_specs=in_specs,
        out_specs=out_specs,
        out_shape=out_shape,
        compiler_params=_params("arbitrary"),
        name="post",
    )(*args)
    return (out[0], out[1]) if has_next else (out[0], None)


def _round_kernel(w_ref, o_ref):
    o_ref[...] = w_ref[...].astype(o_ref.dtype)


def _weight_round_rider(w_all, layer, grid_rows, tn):
    k = w_all.shape[1]
    rows = k // grid_rows
    assert rows * grid_rows == k and rows % (2 * SUBLANES) == 0
    return Rider(_round_kernel,
                 [pl.BlockSpec((None, rows, tn), lambda i, j: (layer, i, j))], [w_all],
                 [pl.BlockSpec((rows, tn), lambda i, j: (i, j))], [jax.ShapeDtypeStruct(w_all.shape[1:], BF16)])


def _rider_lists(riders):
    cat = lambda field: [v for r in riders for v in getattr(r, field)]
    return cat("in_specs"), cat("args"), cat("out_specs"), cat("out_shape")


def _in_proj_kernel(x_ref, w_ref, *rest, riders):
    n_in = sum(len(r.in_specs) for r in riders)
    o_ref = rest[n_in]
    o_ref[...] = jnp.dot(x_ref[...], w_ref[...], preferred_element_type=F32).astype(o_ref.dtype)
    _run_riders(riders, rest[:n_in], rest[n_in + 1:])


def _proj_tiles(t, n):
    return _tile(t, MM_TILE), _tile(n, MM_TILE)


def _in_proj(hn, w, out_dtype, riders=()):
    t, d = hn.shape
    n = w.shape[1]
    tm, tn = _proj_tiles(t, n)
    r_in, r_args, r_out, r_shape = _rider_lists(riders)
    return pl.pallas_call(
        functools.partial(_in_proj_kernel, riders=riders),
        grid=(t // tm, n // tn),
        in_specs=[pl.BlockSpec((tm, d), lambda i, j: (i, 0)), pl.BlockSpec((d, tn), lambda i, j: (0, j)), *r_in],
        out_specs=[pl.BlockSpec((tm, tn), lambda i, j: (i, j)), *r_out],
        out_shape=[jax.ShapeDtypeStruct((t, n), out_dtype), *r_shape],
        compiler_params=_params("arbitrary", "arbitrary"),
        name="in_proj",
    )(hn, w, *r_args)


def _out_proj_kernel(a_ref, b_ref, c_ref, d_ref, w_ref, *rest, riders):
    n_in = sum(len(r.in_specs) for r in riders)
    o_ref, cat_ref = rest[n_in], rest[-1]
    wb = a_ref.shape[1]

    @pl.when(pl.program_id(1) == 0)
    def _():
        for s, ref in enumerate((a_ref, b_ref, c_ref, d_ref)):
            cat_ref[:, s * wb:(s + 1) * wb] = ref[...]

    o_ref[...] = jnp.dot(cat_ref[...], w_ref[...], preferred_element_type=F32).astype(o_ref.dtype)
    _run_riders(riders, rest[:n_in], rest[n_in + 1:-1])


def _out_proj(branches, w, riders=()):
    t, wb = branches[0].shape
    k, n = w.shape
    tm, tn = _proj_tiles(t, n)
    slab = pl.BlockSpec((tm, wb), lambda i, j: (i, 0))
    r_in, r_args, r_out, r_shape = _rider_lists(riders)
    return pl.pallas_call(
        functools.partial(_out_proj_kernel, riders=riders),
        grid=(t // tm, n // tn),
        in_specs=[slab, slab, slab, slab, pl.BlockSpec((k, tn), lambda i, j: (0, j)), *r_in],
        out_specs=[pl.BlockSpec((tm, tn), lambda i, j: (i, j)), *r_out],
        out_shape=[jax.ShapeDtypeStruct((t, n), BF16), *r_shape],
        scratch_shapes=[pltpu.VMEM((tm, k), BF16)],
        compiler_params=_params("arbitrary", "arbitrary"),
        name="out_proj",
    )(*branches, w, *r_args)


def _seq_specs(tl, wb, total_rows, cols):
    r = tl // HALO
    last = total_rows // HALO - 1
    main = pl.BlockSpec((tl, wb), lambda i: (i, cols))
    prev = pl.BlockSpec((HALO, wb), lambda i: (jnp.maximum(i * r - 1, 0), cols))
    nxt = pl.BlockSpec((HALO, wb), lambda i: (jnp.minimum((i + 1) * r, last), cols))
    return main, prev, nxt


def _fill_halo(z_ref, fn, main, prev, nxt, tl, tiles_per_seq):
    i = pl.program_id(0) % tiles_per_seq
    z_ref[0:HALO, :] = jnp.where(i != 0, fn(*[r[...].astype(F32) for r in prev]), 0.0)
    z_ref[HALO:HALO + tl, :] = fn(*[r[...].astype(F32) for r in main])
    z_ref[HALO + tl:, :] = jnp.where(i != tiles_per_seq - 1, fn(*[r[...].astype(F32) for r in nxt]), 0.0)


def _conv_block(z_ref, cw_ref, r0, rc, lanes):
    taps = cw_ref.shape[0]
    first = HALO - (taps - 1) // 2
    win = z_ref[pl.ds(r0, rc + 2 * HALO), lanes]
    acc = None
    for s in range(SUBLANES):
        offs = [o for o in range(first, first + taps) if o % SUBLANES == s]
        if offs:
            rolled = pltpu.roll(win, win.shape[0] - s, 0) if s else win
            for o in offs:
                term = cw_ref[o - first:o - first + 1, lanes] * rolled[o - s:o - s + rc]
                acc = term if acc is None else acc + term
    return acc


def _conformer_kernel(val, val_p, val_n, glu, glu_p, glu_n, gate_ref, cw_ref, cb_ref, g_ref, b_ref, o_ref, z_ref,
                      y_ref, *, tl, tiles_per_seq):
    _fill_halo(z_ref, lambda v, g: v * jax.nn.sigmoid(g), (val, glu), (val_p, glu_p), (val_n, glu_n), tl, tiles_per_seq)
    wb = o_ref.shape[1]
    rc = min(CONV_ROWS, tl)

    def chunk(ci, carry):
        r0 = pl.multiple_of(ci * rc, rc)
        for cb in range(wb // LANES):
            lanes = slice(cb * LANES, (cb + 1) * LANES)
            y_ref[pl.ds(r0, rc), lanes] = _conv_block(z_ref, cw_ref, r0, rc, lanes) + cb_ref[:, lanes]
        y = _silu(_ln(y_ref[pl.ds(r0, rc), :], g_ref[...], b_ref[...]))
        o_ref[pl.ds(r0, rc), :] = (y * _silu(gate_ref[pl.ds(r0, rc), :].astype(F32))).astype(o_ref.dtype)
        return carry

    lax.fori_loop(0, tl // rc, chunk, 0)


def _branch_a(p, seq_len, conv_w, conv_b, ln_g, ln_b, layer):
    t = p.shape[0]
    wb = p.shape[1] // N_IN_SLICES
    tl = _tile(seq_len, SEQ_TILE)
    taps = conv_w.shape[1]
    v_specs = _seq_specs(tl, wb, t, 0)
    g_specs = _seq_specs(tl, wb, t, 1)
    vec = _vec_spec(layer, wb)
    return pl.pallas_call(
        functools.partial(_conformer_kernel, tl=tl, tiles_per_seq=seq_len // tl),
        grid=(t // tl,),
        in_specs=[*v_specs, *g_specs, pl.BlockSpec((tl, wb), lambda i: (i, 2)),
                  pl.BlockSpec((None, taps, wb), lambda i: (layer, 0, 0)), vec, vec, vec],
        out_specs=pl.BlockSpec((tl, wb), lambda i: (i, 0)),
        out_shape=jax.ShapeDtypeStruct((t, wb), BF16),
        scratch_shapes=[pltpu.VMEM((tl + 2 * HALO, wb), F32), pltpu.VMEM((tl, wb), F32)],
        compiler_params=_params("arbitrary"),
        name="branch_a_conformer",
    )(p, p, p, p, p, p, p, conv_w, conv_b, ln_g, ln_b)


def _gmlp_kernel(u_ref, v_ref, gate_ref, g_ref, b_ref, ws_ref, bias_ref, o_ref, *, tl):
    nh, chunk = ws_ref.shape[0], ws_ref.shape[1]
    hd = o_ref.shape[1] // nh
    vn = _ln(v_ref[...].astype(F32), g_ref[...], b_ref[...]).astype(BF16)
    for c in range(tl // chunk):
        rows = slice(c * chunk, (c + 1) * chunk)
        for h in range(nh):
            cols = slice(h * hd, (h + 1) * hd)
            s = jnp.dot(ws_ref[h], vn[rows, cols], preferred_element_type=F32) + bias_ref[:, cols]
            y = u_ref[rows, cols].astype(F32) * s * _silu(gate_ref[rows, cols].astype(F32))
            o_ref[rows, cols] = y.astype(o_ref.dtype)


def _branch_b(p, seq_len, ln_g, ln_b, ws, bias_full, layer):
    t = p.shape[0]
    wb = p.shape[1] // N_IN_SLICES
    nh, chunk = ws.shape[1], ws.shape[2]
    tl = _tile(seq_len, SEQ_TILE)
    vec = _vec_spec(layer, wb)
    return pl.pallas_call(
        functools.partial(_gmlp_kernel, tl=tl),
        grid=(t // tl,),
        in_specs=[pl.BlockSpec((tl, wb), lambda i: (i, 3)), pl.BlockSpec((tl, wb), lambda i: (i, 4)),
                  pl.BlockSpec((tl, wb), lambda i: (i, 5)), vec, vec,
                  pl.BlockSpec((None, nh, chunk, chunk), lambda i: (layer, 0, 0, 0)),
                  pl.BlockSpec((None, chunk, wb), lambda i: (layer, 0, 0))],
        out_specs=pl.BlockSpec((tl, wb), lambda i: (i, 0)),
        out_shape=jax.ShapeDtypeStruct((t, wb), BF16),
        compiler_params=_params("arbitrary"),
        name="branch_b_gmlp",
    )(p, p, p, ln_g, ln_b, ws, bias_full)


def _short_conv_kernel(cc, cc_p, cc_n, cx, cx_p, cx_n, cb_ref, gate_ref, cw_ref, o_ref, z_ref, *, tl, tiles_per_seq):
    _fill_halo(z_ref, lambda a, b: a * b, (cc, cx), (cc_p, cx_p), (cc_n, cx_n), tl, tiles_per_seq)
    wb = o_ref.shape[1]
    rc = min(CONV_ROWS, tl)

    def chunk(ci, carry):
        r0 = pl.multiple_of(ci * rc, rc)
        for blk in range(wb // LANES):
            lanes = slice(blk * LANES, (blk + 1) * LANES)
            y = cb_ref[pl.ds(r0, rc), lanes].astype(F32) * _conv_block(z_ref, cw_ref, r0, rc, lanes)
            o_ref[pl.ds(r0, rc), lanes] = (y * _silu(gate_ref[pl.ds(r0, rc), lanes].astype(F32))).astype(o_ref.dtype)
        return carry

    lax.fori_loop(0, tl // rc, chunk, 0)


def _branch_c(p, seq_len, conv_w, layer):
    t = p.shape[0]
    wb = p.shape[1] // N_IN_SLICES
    tl = _tile(seq_len, SEQ_TILE)
    taps = conv_w.shape[1]
    return pl.pallas_call(
        functools.partial(_short_conv_kernel, tl=tl, tiles_per_seq=seq_len // tl),
        grid=(t // tl,),
        in_specs=[*_seq_specs(tl, wb, t, 7), *_seq_specs(tl, wb, t, 8),
                  pl.BlockSpec((tl, wb), lambda i: (i, 6)), pl.BlockSpec((tl, wb), lambda i: (i, 9)),
                  pl.BlockSpec((None, taps, wb), lambda i: (layer, 0, 0))],
        out_specs=pl.BlockSpec((tl, wb), lambda i: (i, 0)),
        out_shape=jax.ShapeDtypeStruct((t, wb), BF16),
        scratch_shapes=[pltpu.VMEM((tl + 2 * HALO, wb), F32)],
        compiler_params=_params("arbitrary"),
        name="branch_c_short_conv",
    )(p, p, p, p, p, p, p, p, conv_w)


_NT = (((1,), (1,)), ((), ()))


def _ctx_attn_kernel(q_ref, k_ref, v_ref, gate_ref, kall_ref, vall_ref, o_ref, knew_ref, vnew_ref, *, nh):
    del kall_ref, vall_ref
    knew_ref[...] = k_ref[...].astype(knew_ref.dtype)
    vnew_ref[...] = v_ref[...].astype(vnew_ref.dtype)
    hd = o_ref.shape[1] // nh
    scale = hd ** -0.5
    for h in range(nh):
        cols = slice(h * hd, (h + 1) * hd)
        q = q_ref[:, cols].astype(BF16)
        s = lax.dot_general(q, k_ref[:, cols].astype(BF16), _NT, preferred_element_type=F32) * scale
        e = jnp.exp(s - jnp.max(s, axis=-1, keepdims=True))
        o = jnp.dot(e.astype(BF16), v_ref[:, cols].astype(BF16), preferred_element_type=F32)
        o = o / jnp.sum(e, axis=-1, keepdims=True)
        o_ref[:, cols] = (o * _silu(gate_ref[:, cols].astype(F32))).astype(o_ref.dtype)


def _branch_d_context(p, seq_len, nh, layer, new_k, new_v):
    t = p.shape[0]
    wb = p.shape[1] // N_IN_SLICES
    kv_out = pl.BlockSpec((None, None, seq_len, wb), lambda b: (b, layer, 0, 0))
    any_space = pl.BlockSpec(memory_space=pl.ANY)
    return pl.pallas_call(
        functools.partial(_ctx_attn_kernel, nh=nh),
        grid=(t // seq_len,),
        in_specs=[*[pl.BlockSpec((seq_len, wb), lambda b, s=s: (b, s)) for s in (10, 11, 12, 13)], any_space, any_space],
        out_specs=[pl.BlockSpec((seq_len, wb), lambda b: (b, 0)), kv_out, kv_out],
        out_shape=[jax.ShapeDtypeStruct((t, wb), BF16), jax.ShapeDtypeStruct(new_k.shape, new_k.dtype),
                   jax.ShapeDtypeStruct(new_v.shape, new_v.dtype)],
        input_output_aliases={4: 1, 5: 2},
        compiler_params=_params("arbitrary"),
        name="branch_d_context_attention",
    )(p, p, p, p, new_k, new_v)


def _nbr_attn_kernel(q_ref, k_ref, v_ref, gate_ref, ck_ref, cv_ref, bias_ref, o_ref, oc_ref, mc_ref, lc_ref,
                     *, rows, kr, hd):
    for h in range(o_ref.shape[1] // hd):
        cols = slice(h * hd, (h + 1) * hd)
        _nbr_attn_head(q_ref, k_ref, v_ref, gate_ref, ck_ref, cv_ref, bias_ref.at[h], o_ref, oc_ref, mc_ref, lc_ref,
                       cols, rows, kr)


def _nbr_attn_head(q_ref, k_ref, v_ref, gate_ref, ck_ref, cv_ref, bias_ref, o_ref, oc_ref, mc_ref, lc_ref,
                   cols, rows, kr):
    hd = cols.stop - cols.start
    seq_len = q_ref.shape[0]
    win_r = bias_ref.shape[0] // 2 + 1
    scale = hd ** -0.5
    ck = ck_ref[:, cols].astype(BF16)
    cv = cv_ref[:, cols].astype(BF16)

    cq = _tile(seq_len, 512)

    def ctx_chunk(ci, carry):
        r0 = pl.multiple_of(ci * cq, cq)
        s = lax.dot_general(q_ref[pl.ds(r0, cq), cols].astype(BF16), ck, _NT, preferred_element_type=F32) * scale
        m = jnp.max(s, axis=-1, keepdims=True)
        e = jnp.exp(s - m)
        mc_ref[pl.ds(r0, cq), :] = m
        lc_ref[pl.ds(r0, cq), :] = jnp.sum(e, axis=-1, keepdims=True)
        oc_ref[pl.ds(r0, cq), :] = jnp.dot(e.astype(BF16), cv, preferred_element_type=F32)
        return carry

    lax.fori_loop(0, seq_len // cq, ctx_chunk, 0, unroll=2 if (seq_len // cq) % 2 == 0 else 1)

    group = ATTN_ROW_GROUP if rows % ATTN_ROW_GROUP == 0 else 1

    def window_starts(gi):
        rs = [gi * group + i for i in range(group)]
        starts = [jnp.clip(r - kr // 2, 0, rows - kr) for r in rs]
        return rs, starts, [pl.multiple_of(s * GRID_W, GRID_W) for s in starts]

    def scores_stage(gi):
        rs, starts, k0s = window_starts(gi)
        out = []
        for r, s, k0 in zip(rs, starts, k0s):
            q0 = pl.multiple_of(r * GRID_W, GRID_W)
            base = win_r - 1 - (r - s)
            bias = jnp.concatenate([bias_ref[base + 2 * t] for t in range(kr // 2)], axis=1)
            out.append(lax.dot_general(q_ref[pl.ds(q0, GRID_W), cols].astype(BF16),
                                       k_ref[pl.ds(k0, kr * GRID_W), cols].astype(BF16), _NT,
                                       preferred_element_type=F32) * scale + bias)
        return tuple(out)

    def softmax_stage(scores):
        out = []
        for sc in scores:
            m = jnp.max(sc, axis=-1, keepdims=True)
            e = jnp.exp(sc - m)
            out.append((e.astype(BF16), m, jnp.sum(e, axis=-1, keepdims=True)))
        return tuple(out)

    def values_stage(gi, stats):
        rs, _, k0s = window_starts(gi)
        for r, k0, (e, m_l, l_l) in zip(rs, k0s, stats):
            q0 = pl.multiple_of(r * GRID_W, GRID_W)
            o_l = jnp.dot(e, v_ref[pl.ds(k0, kr * GRID_W), cols].astype(BF16), preferred_element_type=F32)
            m_c = mc_ref[pl.ds(q0, GRID_W), :]
            m = jnp.maximum(m_l, m_c)
            a_l = jnp.exp(m_l - m)
            a_c = jnp.exp(m_c - m)
            o = (a_l * o_l + a_c * oc_ref[pl.ds(q0, GRID_W), :]) / (a_l * l_l + a_c * lc_ref[pl.ds(q0, GRID_W), :])
            g = gate_ref[pl.ds(q0, GRID_W), cols].astype(F32)
            o_ref[pl.ds(q0, GRID_W), cols] = (o * _silu(g)).astype(o_ref.dtype)

    def pipelined(gi, stats):
        nxt = softmax_stage(scores_stage(gi + 1))
        values_stage(gi, stats)
        return nxt

    n_groups = rows // group
    last = lax.fori_loop(0, n_groups - 1, pipelined, softmax_stage(scores_stage(jnp.int32(0))))
    values_stage(jnp.int32(n_groups - 1), last)


def _branch_d_latent(p, seq_len, cache_k, cache_v, bias_tab, layer):
    t = p.shape[0]
    wb = p.shape[1] // N_IN_SLICES
    nh, n_off = bias_tab.shape[1], bias_tab.shape[2]
    hd = wb // nh
    past = cache_k.shape[2]
    rows = seq_len // GRID_W
    kr = min(n_off // 2 + 1, rows)
    assert kr % 2 == 0
    hp = ATTN_HEADS_PER_STEP if nh % ATTN_HEADS_PER_STEP == 0 else 1
    steps = nh // hp
    col = lambda s: pl.BlockSpec((seq_len, hp * hd), lambda b, h, s=s: (b, s * steps + h))
    ctx = pl.BlockSpec((None, None, past, hp * hd), lambda b, h: (b, layer, 0, h))
    return pl.pallas_call(
        functools.partial(_nbr_attn_kernel, rows=rows, kr=kr, hd=hd),
        grid=(t // seq_len, steps),
        in_specs=[col(10), col(11), col(12), col(13), ctx, ctx,
                  pl.BlockSpec((None, hp, n_off, GRID_W, 2 * GRID_W), lambda b, h: (layer, h, 0, 0, 0))],
        out_specs=pl.BlockSpec((seq_len, hp * hd), lambda b, h: (b, h)),
        out_shape=jax.ShapeDtypeStruct((t, wb), BF16),
        scratch_shapes=[pltpu.VMEM((seq_len, hd), F32), pltpu.VMEM((seq_len, 1), F32), pltpu.VMEM((seq_len, 1), F32)],
        compiler_params=_params("arbitrary", "arbitrary"),
        name="branch_d_neighbourhood_attention",
    )(p, p, p, p, cache_k, cache_v, bias_tab)


def _neighbourhood_bias(rpb):
    win_c = (rpb.shape[3] + 1) // 2
    qc = np.arange(GRID_W)[:, None]
    kc = np.arange(GRID_W)[None, :]
    sc = np.clip(qc - win_c // 2, 0, GRID_W - win_c)
    valid = (kc >= sc) & (kc < sc + win_c)
    select = (kc - qc + win_c - 1 == np.arange(2 * win_c - 1)[:, None, None]) & valid
    cols = jnp.einsum("lhrd,dqk->lhrqk", rpb.astype(F32), jnp.asarray(select, F32), precision=lax.Precision.HIGHEST)
    cols = jnp.where(valid, cols, NEG)
    return jnp.concatenate([cols[:, :, :-1], cols[:, :, 1:]], axis=-1)


def kernel(x_prompt, x_sample, cache_k, cache_v, c, c_ctx, w_ada, b_ada, g_pre, g_post, w_in, a_conv_w, a_conv_b,
           a_ln_g, a_ln_b, b_ln_g, b_ln_b, b_ws, b_bias, c_conv_w, d_rpb, w_out):
    batch, seq, d = x_prompt.shape
    dec_batch, dec_seq, _ = x_sample.shape
    depth = w_in.shape[0]
    wb = d // 4
    nh, hd = cache_k.shape[3], cache_k.shape[4]
    past = cache_k.shape[2]
    assert nh * hd == wb and w_in.shape[2] == N_IN_SLICES * wb and dec_seq % GRID_W == 0

    n_rows = -(-(1 + dec_batch) // 8) * 8
    cond = jnp.zeros((n_rows, d), F32).at[0].set(c_ctx).at[1:1 + dec_batch].set(c)
    mod = _ada_modulation(cond, w_ada, b_ada).reshape(depth, n_rows, 3, 1, d)

    vec = lambda a: a.reshape(depth, 1, a.shape[-1])
    g_pre, g_post, a_conv_b, a_ln_g, a_ln_b, b_ln_g, b_ln_b = map(
        vec, (g_pre, g_post, a_conv_b, a_ln_g, a_ln_b, b_ln_g, b_ln_b))
    ws_b = b_ws.astype(BF16)
    gh = wb // b_ws.shape[1]
    bias_full = jnp.repeat(jnp.swapaxes(b_bias, 1, 2), gh, axis=2)
    bias_tab = _neighbourhood_bias(d_rpb)
    ck = cache_k.reshape(dec_batch, depth, past, wb)
    cv = cache_v.reshape(dec_batch, depth, past, wb)

    xs = [x_prompt.reshape(batch * seq, d), x_sample.reshape(dec_batch * dec_seq, d)]
    seqs = [seq, dec_seq]
    row0 = [0, 1]
    mod_seq = [batch * seq, dec_seq]
    p_dtype = [F32, BF16]
    hn = [_prenorm(xs[i], g_pre, mod, 0, row0[i], mod_seq[i]) for i in range(2)]
    new_k = jnp.zeros((batch, depth, seq, wb), F32)
    new_v = jnp.zeros((batch, depth, seq, wb), F32)
    w_in_l, w_out_l = w_in[0].astype(BF16), w_out[0].astype(BF16)
    lat_rows = xs[1].shape[0]
    lat_row_tiles = lat_rows // _proj_tiles(lat_rows, d)[0]
    for l in range(depth):
        has_next = l + 1 < depth

        def branches(i, p):
            ya = _branch_a(p, seqs[i], a_conv_w, a_conv_b, a_ln_g, a_ln_b, l)
            yb = _branch_b(p, seqs[i], b_ln_g, b_ln_b, ws_b, bias_full, l)
            yc = _branch_c(p, seqs[i], c_conv_w, l)
            return ya, yb, yc

        p = _in_proj(hn[0], w_in_l, p_dtype[0])[0]
        yd, new_k, new_v = _branch_d_context(p, seq, nh, l, new_k, new_v)
        y = _out_proj((*branches(0, p), yd), w_out_l)[0]
        xs[0], hn[0] = _post(xs[0], y, g_post, g_pre, mod, l, row0[0], mod_seq[0], has_next)

        rider = lambda w: (_weight_round_rider(w, l + 1, lat_row_tiles, _proj_tiles(lat_rows, w.shape[2])[1]),) \
            if has_next else ()
        p, *w_in_next = _in_proj(hn[1], w_in_l, p_dtype[1], rider(w_in))
        yd = _branch_d_latent(p, dec_seq, ck, cv, bias_tab, l)
        y, *w_out_next = _out_proj((*branches(1, p), yd), w_out_l, rider(w_out))
        xs[1], hn[1] = _post(xs[1], y, g_post, g_pre, mod, l, row0[1], mod_seq[1], has_next)
        if has_next:
            w_in_l, w_out_l = w_in_next[0], w_out_next[0]
    return (xs[0].reshape(batch, seq, d), xs[1].reshape(dec_batch, dec_seq, d),
            new_k.reshape(batch, depth, seq, nh, hd), new_v.reshape(batch, depth, seq, nh, hd))
```

```python
import collections
import functools

import jax
import jax.numpy as jnp
import numpy as np
from jax import lax
from jax.experimental import pallas as pl
from jax.experimental.pallas import tpu as pltpu

GRID_W = 64
EPS = 1e-6
NEG = -1e30
N_IN_SLICES = 14
HALO = 16
VMEM_LIMIT_BYTES = 56 * 1024 * 1024
MM_TILE = 1024
SEQ_TILE = 512
CONV_ROWS = 64
ATTN_HEADS_PER_STEP = 2
ATTN_ROW_GROUP = 4
LANES = 128
SUBLANES = 8

F32 = jnp.float32
BF16 = jnp.bfloat16


def _tile(n, target, unit=128):
    if n <= target:
        return n
    t = target - target % unit
    while n % t:
        t -= unit
    return t


def _params(*sem):
    return pltpu.CompilerParams(dimension_semantics=sem, vmem_limit_bytes=VMEM_LIMIT_BYTES)


def _silu(x):
    return x * jax.nn.sigmoid(x)


def _rms(x, g):
    return x * lax.rsqrt(jnp.mean(x * x, axis=-1, keepdims=True) + EPS) * g


def _ln(x, g, b):
    mu = jnp.mean(x, axis=-1, keepdims=True)
    xc = x - mu
    var = jnp.mean(xc * xc, axis=-1, keepdims=True)
    return xc * lax.rsqrt(var + EPS) * g + b


def _ada_kernel(cond_ref, w_ref, b_ref, o_ref):
    a = _silu(cond_ref[...]).astype(BF16)
    o_ref[...] = jnp.dot(a, w_ref[...].astype(BF16), preferred_element_type=F32) + b_ref[...]


def _ada_modulation(cond, w_ada, b_ada):
    depth, d, n = w_ada.shape
    r = cond.shape[0]
    tn = _tile(n, 512)
    return pl.pallas_call(
        _ada_kernel,
        grid=(depth, n // tn),
        in_specs=[pl.BlockSpec((r, d), lambda l, j: (0, 0)),
                  pl.BlockSpec((None, d, tn), lambda l, j: (l, 0, j)),
                  pl.BlockSpec((None, 1, tn), lambda l, j: (l, 0, j))],
        out_specs=pl.BlockSpec((None, r, tn), lambda l, j: (l, 0, j)),
        out_shape=jax.ShapeDtypeStruct((depth, r, n), F32),
        compiler_params=_params("arbitrary", "arbitrary"),
        name="ada_modulation",
    )(cond, w_ada, b_ada.reshape(depth, 1, n))


def _prenorm_kernel(x_ref, g_ref, mod_ref, o_ref):
    y = _rms(x_ref[...], g_ref[...])
    o_ref[...] = (y * (1.0 + mod_ref[1]) + mod_ref[0]).astype(o_ref.dtype)


def _post_kernel(x_ref, y_ref, gpost_ref, mod_ref, *rest, has_next):
    xn = x_ref[...] + mod_ref[2] * _rms(y_ref[...].astype(F32), gpost_ref[...])
    if has_next:
        gpre_ref, modn_ref, xo_ref, hn_ref = rest
        hn_ref[...] = (_rms(xn, gpre_ref[...]) * (1.0 + modn_ref[1]) + modn_ref[0]).astype(hn_ref.dtype)
    else:
        (xo_ref,) = rest
    xo_ref[...] = xn


def _row_tile(t):
    return min(256, t)


def _vec_spec(layer, width):
    return pl.BlockSpec((None, 1, width), lambda *_: (layer, 0, 0))


def _mod_spec(layer, row0, tiles_per_row, d):
    return pl.BlockSpec((None, None, 3, 1, d), lambda i: (layer, row0 + i // tiles_per_row, 0, 0, 0))


Rider = collections.namedtuple("Rider", "fn in_specs args out_specs out_shape")


def _run_riders(riders, in_refs, out_refs):
    for r in riders:
        n_in, n_out = len(r.in_specs), len(r.out_specs)
        r.fn(*in_refs[:n_in], *out_refs[:n_out])
        in_refs, out_refs = in_refs[n_in:], out_refs[n_out:]


def _prenorm(x, g, mod, layer, row0, seq_len):
    t, d = x.shape
    tm = _row_tile(seq_len)
    return pl.pallas_call(
        _prenorm_kernel,
        grid=(t // tm,),
        in_specs=[pl.BlockSpec((tm, d), lambda i: (i, 0)),
                  _vec_spec(layer, d),
                  _mod_spec(layer, row0, seq_len // tm, d)],
        out_specs=pl.BlockSpec((tm, d), lambda i: (i, 0)),
        out_shape=jax.ShapeDtypeStruct((t, d), BF16),
        compiler_params=_params("arbitrary"),
        name="prenorm",
    )(x, g, mod)


def _post(x, y, g_post, g_pre, mod, layer, row0, seq_len, has_next):
    t, d = x.shape
    tm = _row_tile(seq_len)
    tpr = seq_len // tm
    row = pl.BlockSpec((tm, d), lambda i: (i, 0))
    in_specs = [row, row, _vec_spec(layer, d), _mod_spec(layer, row0, tpr, d)]
    args = [x, y, g_post, mod]
    out_specs = [row]
    out_shape = [jax.ShapeDtypeStruct((t, d), F32)]
    if has_next:
        in_specs += [_vec_spec(layer + 1, d), _mod_spec(layer + 1, row0, tpr, d)]
        args += [g_pre, mod]
        out_specs.append(row)
        out_shape.append(jax.ShapeDtypeStruct((t, d), BF16))
    out = pl.pallas_call(
        functools.partial(_post_kernel, has_next=has_next),
        grid=(t // tm,),
        in_specs=in_specs,
        out_specs=out_specs,
        out_shape=out_shape,
        compiler_params=_params("arbitrary"),
        name="post",
    )(*args)
    return (out[0], out[1]) if has_next else (out[0], None)


def _round_kernel(w_ref, o_ref):
    o_ref[...] = w_ref[...].astype(o_ref.dtype)


def _weight_round_rider(w_all, layer, grid_rows, tn):
    k = w_all.shape[1]
    rows = k // grid_rows
    assert rows * grid_rows == k and rows % (2 * SUBLANES) == 0
    return Rider(_round_kernel,
                 [pl.BlockSpec((None, rows, tn), lambda i, j: (layer, i, j))], [w_all],
                 [pl.BlockSpec((rows, tn), lambda i, j: (i, j))], [jax.ShapeDtypeStruct(w_all.shape[1:], BF16)])


def _rider_lists(riders):
    cat = lambda field: [v for r in riders for v in getattr(r, field)]
    return cat("in_specs"), cat("args"), cat("out_specs"), cat("out_shape")


def _in_proj_kernel(x_ref, w_ref, *rest, riders):
    n_in = sum(len(r.in_specs) for r in riders)
    o_ref = rest[n_in]
    o_ref[...] = jnp.dot(x_ref[...], w_ref[...], preferred_element_type=F32).astype(o_ref.dtype)
    _run_riders(riders, rest[:n_in], rest[n_in + 1:])


def _proj_tiles(t, n):
    return _tile(t, MM_TILE), _tile(n, MM_TILE)


def _in_proj(hn, w, out_dtype, riders=()):
    t, d = hn.shape
    n = w.shape[1]
    tm, tn = _proj_tiles(t, n)
    r_in, r_args, r_out, r_shape = _rider_lists(riders)
    return pl.pallas_call(
        functools.partial(_in_proj_kernel, riders=riders),
        grid=(t // tm, n // tn),
        in_specs=[pl.BlockSpec((tm, d), lambda i, j: (i, 0)), pl.BlockSpec((d, tn), lambda i, j: (0, j)), *r_in],
        out_specs=[pl.BlockSpec((tm, tn), lambda i, j: (i, j)), *r_out],
        out_shape=[jax.ShapeDtypeStruct((t, n), out_dtype), *r_shape],
        compiler_params=_params("arbitrary", "arbitrary"),
        name="in_proj",
    )(hn, w, *r_args)


def _out_proj_kernel(a_ref, b_ref, c_ref, d_ref, w_ref, *rest, riders):
    n_in = sum(len(r.in_specs) for r in riders)
    o_ref, cat_ref = rest[n_in], rest[-1]
    wb = a_ref.shape[1]

    @pl.when(pl.program_id(1) == 0)
    def _():
        for s, ref in enumerate((a_ref, b_ref, c_ref, d_ref)):
            cat_ref[:, s * wb:(s + 1) * wb] = ref[...]

    o_ref[...] = jnp.dot(cat_ref[...], w_ref[...], preferred_element_type=F32).astype(o_ref.dtype)
    _run_riders(riders, rest[:n_in], rest[n_in + 1:-1])


def _out_proj(branches, w, riders=()):
    t, wb = branches[0].shape
    k, n = w.shape
    tm, tn = _proj_tiles(t, n)
    slab = pl.BlockSpec((tm, wb), lambda i, j: (i, 0))
    r_in, r_args, r_out, r_shape = _rider_lists(riders)
    return pl.pallas_call(
        functools.partial(_out_proj_kernel, riders=riders),
        grid=(t // tm, n // tn),
        in_specs=[slab, slab, slab, slab, pl.BlockSpec((k, tn), lambda i, j: (0, j)), *r_in],
        out_specs=[pl.BlockSpec((tm, tn), lambda i, j: (i, j)), *r_out],
        out_shape=[jax.ShapeDtypeStruct((t, n), BF16), *r_shape],
        scratch_shapes=[pltpu.VMEM((tm, k), BF16)],
        compiler_params=_params("arbitrary", "arbitrary"),
        name="out_proj",
    )(*branches, w, *r_args)


def _seq_specs(tl, wb, total_rows, cols):
    r = tl // HALO
    last = total_rows // HALO - 1
    main = pl.BlockSpec((tl, wb), lambda i: (i, cols))
    prev = pl.BlockSpec((HALO, wb), lambda i: (jnp.maximum(i * r - 1, 0), cols))
    nxt = pl.BlockSpec((HALO, wb), lambda i: (jnp.minimum((i + 1) * r, last), cols))
    return main, prev, nxt


def _fill_halo(z_ref, fn, main, prev, nxt, tl, tiles_per_seq):
    i = pl.program_id(0) % tiles_per_seq
    z_ref[0:HALO, :] = jnp.where(i != 0, fn(*[r[...].astype(F32) for r in prev]), 0.0)
    z_ref[HALO:HALO + tl, :] = fn(*[r[...].astype(F32) for r in main])
    z_ref[HALO + tl:, :] = jnp.where(i != tiles_per_seq - 1, fn(*[r[...].astype(F32) for r in nxt]), 0.0)


def _conv_block(z_ref, cw_ref, r0, rc, lanes):
    taps = cw_ref.shape[0]
    first = HALO - (taps - 1) // 2
    win = z_ref[pl.ds(r0, rc + 2 * HALO), lanes]
    acc = None
    for s in range(SUBLANES):
        offs = [o for o in range(first, first + taps) if o % SUBLANES == s]
        if offs:
            rolled = pltpu.roll(win, win.shape[0] - s, 0) if s else win
            for o in offs:
                term = cw_ref[o - first:o - first + 1, lanes] * rolled[o - s:o - s + rc]
                acc = term if acc is None else acc + term
    return acc


def _conformer_kernel(val, val_p, val_n, glu, glu_p, glu_n, gate_ref, cw_ref, cb_ref, g_ref, b_ref, o_ref, z_ref,
                      y_ref, *, tl, tiles_per_seq):
    _fill_halo(z_ref, lambda v, g: v * jax.nn.sigmoid(g), (val, glu), (val_p, glu_p), (val_n, glu_n), tl, tiles_per_seq)
    wb = o_ref.shape[1]
    rc = min(CONV_ROWS, tl)

    def chunk(ci, carry):
        r0 = pl.multiple_of(ci * rc, rc)
        for cb in range(wb // LANES):
            lanes = slice(cb * LANES, (cb + 1) * LANES)
            y_ref[pl.ds(r0, rc), lanes] = _conv_block(z_ref, cw_ref, r0, rc, lanes) + cb_ref[:, lanes]
        y = _silu(_ln(y_ref[pl.ds(r0, rc), :], g_ref[...], b_ref[...]))
        o_ref[pl.ds(r0, rc), :] = (y * _silu(gate_ref[pl.ds(r0, rc), :].astype(F32))).astype(o_ref.dtype)
        return carry

    lax.fori_loop(0, tl // rc, chunk, 0)


def _gmlp_kernel(u_ref, v_ref, gate_ref, g_ref, b_ref, ws_ref, bias_ref, o_ref, *, tl):
    nh, chunk = ws_ref.shape[0], ws_ref.shape[1]
    hd = o_ref.shape[1] // nh
    vn = _ln(v_ref[...].astype(F32), g_ref[...], b_ref[...]).astype(BF16)
    for c in range(tl // chunk):
        rows = slice(c * chunk, (c + 1) * chunk)
        for h in range(nh):
            cols = slice(h * hd, (h + 1) * hd)
            s = jnp.dot(ws_ref[h], vn[rows, cols], preferred_element_type=F32) + bias_ref[:, cols]
            y = u_ref[rows, cols].astype(F32) * s * _silu(gate_ref[rows, cols].astype(F32))
            o_ref[rows, cols] = y.astype(o_ref.dtype)


def _short_conv_kernel(cc, cc_p, cc_n, cx, cx_p, cx_n, cb_ref, gate_ref, cw_ref, o_ref, z_ref, *, tl, tiles_per_seq):
    _fill_halo(z_ref, lambda a, b: a * b, (cc, cx), (cc_p, cx_p), (cc_n, cx_n), tl, tiles_per_seq)
    wb = o_ref.shape[1]
    rc = min(CONV_ROWS, tl)

    def chunk(ci, carry):
        r0 = pl.multiple_of(ci * rc, rc)
        for blk in range(wb // LANES):
            lanes = slice(blk * LANES, (blk + 1) * LANES)
            y = cb_ref[pl.ds(r0, rc), lanes].astype(F32) * _conv_block(z_ref, cw_ref, r0, rc, lanes)
            o_ref[pl.ds(r0, rc), lanes] = (y * _silu(gate_ref[pl.ds(r0, rc), lanes].astype(F32))).astype(o_ref.dtype)
        return carry

    lax.fori_loop(0, tl // rc, chunk, 0)


_N_CONFORMER_IN, _N_GMLP_IN, _N_SHORT_CONV_IN = 11, 7, 9


def _seq_branches_kernel(*refs, tl, tiles_per_seq):
    a_in, refs = refs[:_N_CONFORMER_IN], refs[_N_CONFORMER_IN:]
    b_in, refs = refs[:_N_GMLP_IN], refs[_N_GMLP_IN:]
    c_in, (oa_ref, ob_ref, oc_ref, z_ref, y_ref) = refs[:_N_SHORT_CONV_IN], refs[_N_SHORT_CONV_IN:]
    _conformer_kernel(*a_in, oa_ref, z_ref, y_ref, tl=tl, tiles_per_seq=tiles_per_seq)
    _gmlp_kernel(*b_in, ob_ref, tl=tl)
    _short_conv_kernel(*c_in, oc_ref, z_ref, tl=tl, tiles_per_seq=tiles_per_seq)


def _seq_branches(p, seq_len, layer, a_conv_w, a_conv_b, a_ln_g, a_ln_b, b_ln_g, b_ln_b, ws, bias_full, c_conv_w):
    t = p.shape[0]
    wb = p.shape[1] // N_IN_SLICES
    tl = _tile(seq_len, SEQ_TILE)
    nh, chunk = ws.shape[1], ws.shape[2]
    vec = _vec_spec(layer, wb)
    tile = lambda s: pl.BlockSpec((tl, wb), lambda i: (i, s))
    taps = lambda w: pl.BlockSpec((None, w.shape[1], wb), lambda i: (layer, 0, 0))
    a_specs = [*_seq_specs(tl, wb, t, 0), *_seq_specs(tl, wb, t, 1), tile(2), taps(a_conv_w), vec, vec, vec]
    b_specs = [tile(3), tile(4), tile(5), vec, vec,
               pl.BlockSpec((None, nh, chunk, chunk), lambda i: (layer, 0, 0, 0)),
               pl.BlockSpec((None, chunk, wb), lambda i: (layer, 0, 0))]
    c_specs = [*_seq_specs(tl, wb, t, 7), *_seq_specs(tl, wb, t, 8), tile(6), tile(9), taps(c_conv_w)]
    assert (len(a_specs), len(b_specs), len(c_specs)) == (_N_CONFORMER_IN, _N_GMLP_IN, _N_SHORT_CONV_IN)
    slab = pl.BlockSpec((tl, wb), lambda i: (i, 0))
    return pl.pallas_call(
        functools.partial(_seq_branches_kernel, tl=tl, tiles_per_seq=seq_len // tl),
        grid=(t // tl,),
        in_specs=[*a_specs, *b_specs, *c_specs],
        out_specs=[slab, slab, slab],
        out_shape=[jax.ShapeDtypeStruct((t, wb), BF16)] * 3,
        scratch_shapes=[pltpu.VMEM((tl + 2 * HALO, wb), F32), pltpu.VMEM((tl, wb), F32)],
        compiler_params=_params("arbitrary"),
        name="branches_abc",
    )(*[p] * 7, a_conv_w, a_conv_b, a_ln_g, a_ln_b, *[p] * 3, b_ln_g, b_ln_b, ws, bias_full, *[p] * 8, c_conv_w)


_NT = (((1,), (1,)), ((), ()))


def _ctx_attn_kernel(q_ref, k_ref, v_ref, gate_ref, kall_ref, vall_ref, o_ref, knew_ref, vnew_ref, *, nh):
    del kall_ref, vall_ref
    knew_ref[...] = k_ref[...].astype(knew_ref.dtype)
    vnew_ref[...] = v_ref[...].astype(vnew_ref.dtype)
    hd = o_ref.shape[1] // nh
    scale = hd ** -0.5
    for h in range(nh):
        cols = slice(h * hd, (h + 1) * hd)
        q = q_ref[:, cols].astype(BF16)
        s = lax.dot_general(q, k_ref[:, cols].astype(BF16), _NT, preferred_element_type=F32) * scale
        e = jnp.exp(s - jnp.max(s, axis=-1, keepdims=True))
        o = jnp.dot(e.astype(BF16), v_ref[:, cols].astype(BF16), preferred_element_type=F32)
        o = o / jnp.sum(e, axis=-1, keepdims=True)
        o_ref[:, cols] = (o * _silu(gate_ref[:, cols].astype(F32))).astype(o_ref.dtype)


def _branch_d_context(p, seq_len, nh, layer, new_k, new_v):
    t = p.shape[0]
    wb = p.shape[1] // N_IN_SLICES
    kv_out = pl.BlockSpec((None, None, seq_len, wb), lambda b: (b, layer, 0, 0))
    any_space = pl.BlockSpec(memory_space=pl.ANY)
    return pl.pallas_call(
        functools.partial(_ctx_attn_kernel, nh=nh),
        grid=(t // seq_len,),
        in_specs=[*[pl.BlockSpec((seq_len, wb), lambda b, s=s: (b, s)) for s in (10, 11, 12, 13)], any_space, any_space],
        out_specs=[pl.BlockSpec((seq_len, wb), lambda b: (b, 0)), kv_out, kv_out],
        out_shape=[jax.ShapeDtypeStruct((t, wb), BF16), jax.ShapeDtypeStruct(new_k.shape, new_k.dtype),
                   jax.ShapeDtypeStruct(new_v.shape, new_v.dtype)],
        input_output_aliases={4: 1, 5: 2},
        compiler_params=_params("arbitrary"),
        name="branch_d_context_attention",
    )(p, p, p, p, new_k, new_v)


def _nbr_attn_kernel(q_ref, k_ref, v_ref, gate_ref, ck_ref, cv_ref, bias_ref, o_ref, oc_ref, mc_ref, lc_ref,
                     *, rows, kr, hd):
    for h in range(o_ref.shape[1] // hd):
        cols = slice(h * hd, (h + 1) * hd)
        _nbr_attn_head(q_ref, k_ref, v_ref, gate_ref, ck_ref, cv_ref, bias_ref.at[h], o_ref, oc_ref, mc_ref, lc_ref,
                       cols, rows, kr)


def _nbr_attn_head(q_ref, k_ref, v_ref, gate_ref, ck_ref, cv_ref, bias_ref, o_ref, oc_ref, mc_ref, lc_ref,
                   cols, rows, kr):
    hd = cols.stop - cols.start
    seq_len = q_ref.shape[0]
    win_r = bias_ref.shape[0] // 2 + 1
    scale = hd ** -0.5
    ck = ck_ref[:, cols].astype(BF16)
    cv = cv_ref[:, cols].astype(BF16)

    cq = _tile(seq_len, 512)

    def ctx_chunk(ci, carry):
        r0 = pl.multiple_of(ci * cq, cq)
        s = lax.dot_general(q_ref[pl.ds(r0, cq), cols].astype(BF16), ck, _NT, preferred_element_type=F32) * scale
        m = jnp.max(s, axis=-1, keepdims=True)
        e = jnp.exp(s - m)
        mc_ref[pl.ds(r0, cq), :] = m
        lc_ref[pl.ds(r0, cq), :] = jnp.sum(e, axis=-1, keepdims=True)
        oc_ref[pl.ds(r0, cq), :] = jnp.dot(e.astype(BF16), cv, preferred_element_type=F32)
        return carry

    lax.fori_loop(0, seq_len // cq, ctx_chunk, 0, unroll=2 if (seq_len // cq) % 2 == 0 else 1)

    group = ATTN_ROW_GROUP if rows % ATTN_ROW_GROUP == 0 else 1

    def window_starts(gi):
        rs = [gi * group + i for i in range(group)]
        starts = [jnp.clip(r - kr // 2, 0, rows - kr) for r in rs]
        return rs, starts, [pl.multiple_of(s * GRID_W, GRID_W) for s in starts]

    def scores_stage(gi):
        rs, starts, k0s = window_starts(gi)
        out = []
        for r, s, k0 in zip(rs, starts, k0s):
            q0 = pl.multiple_of(r * GRID_W, GRID_W)
            base = win_r - 1 - (r - s)
            bias = jnp.concatenate([bias_ref[base + 2 * t] for t in range(kr // 2)], axis=1)
            out.append(lax.dot_general(q_ref[pl.ds(q0, GRID_W), cols].astype(BF16),
                                       k_ref[pl.ds(k0, kr * GRID_W), cols].astype(BF16), _NT,
                                       preferred_element_type=F32) * scale + bias)
        return tuple(out)

    def softmax_stage(scores):
        out = []
        for sc in scores:
            m = jnp.max(sc, axis=-1, keepdims=True)
            e = jnp.exp(sc - m)
            out.append((e.astype(BF16), m, jnp.sum(e, axis=-1, keepdims=True)))
        return tuple(out)

    def values_stage(gi, stats):
        rs, _, k0s = window_starts(gi)
        for r, k0, (e, m_l, l_l) in zip(rs, k0s, stats):
            q0 = pl.multiple_of(r * GRID_W, GRID_W)
            o_l = jnp.dot(e, v_ref[pl.ds(k0, kr * GRID_W), cols].astype(BF16), preferred_element_type=F32)
            m_c = mc_ref[pl.ds(q0, GRID_W), :]
            m = jnp.maximum(m_l, m_c)
            a_l = jnp.exp(m_l - m)
            a_c = jnp.exp(m_c - m)
            o = (a_l * o_l + a_c * oc_ref[pl.ds(q0, GRID_W), :]) / (a_l * l_l + a_c * lc_ref[pl.ds(q0, GRID_W), :])
            g = gate_ref[pl.ds(q0, GRID_W), cols].astype(F32)
            o_ref[pl.ds(q0, GRID_W), cols] = (o * _silu(g)).astype(o_ref.dtype)

    def pipelined(gi, stats):
        nxt = softmax_stage(scores_stage(gi + 1))
        values_stage(gi, stats)
        return nxt

    n_groups = rows // group
    last = lax.fori_loop(0, n_groups - 1, pipelined, softmax_stage(scores_stage(jnp.int32(0))))
    values_stage(jnp.int32(n_groups - 1), last)


def _branch_d_latent(p, seq_len, cache_k, cache_v, bias_tab, layer):
    t = p.shape[0]
    wb = p.shape[1] // N_IN_SLICES
    nh, n_off = bias_tab.shape[1], bias_tab.shape[2]
    hd = wb // nh
    past = cache_k.shape[2]
    rows = seq_len // GRID_W
    kr = min(n_off // 2 + 1, rows)
    assert kr % 2 == 0
    hp = ATTN_HEADS_PER_STEP if nh % ATTN_HEADS_PER_STEP == 0 else 1
    steps = nh // hp
    col = lambda s: pl.BlockSpec((seq_len, hp * hd), lambda b, h, s=s: (b, s * steps + h))
    ctx = pl.BlockSpec((None, None, past, hp * hd), lambda b, h: (b, layer, 0, h))
    return pl.pallas_call(
        functools.partial(_nbr_attn_kernel, rows=rows, kr=kr, hd=hd),
        grid=(t // seq_len, steps),
        in_specs=[col(10), col(11), col(12), col(13), ctx, ctx,
                  pl.BlockSpec((None, hp, n_off, GRID_W, 2 * GRID_W), lambda b, h: (layer, h, 0, 0, 0))],
        out_specs=pl.BlockSpec((seq_len, hp * hd), lambda b, h: (b, h)),
        out_shape=jax.ShapeDtypeStruct((t, wb), BF16),
        scratch_shapes=[pltpu.VMEM((seq_len, hd), F32), pltpu.VMEM((seq_len, 1), F32), pltpu.VMEM((seq_len, 1), F32)],
        compiler_params=_params("arbitrary", "arbitrary"),
        name="branch_d_neighbourhood_attention",
    )(p, p, p, p, cache_k, cache_v, bias_tab)


def _neighbourhood_bias(rpb):
    win_c = (rpb.shape[3] + 1) // 2
    qc = np.arange(GRID_W)[:, None]
    kc = np.arange(GRID_W)[None, :]
    sc = np.clip(qc - win_c // 2, 0, GRID_W - win_c)
    valid = (kc >= sc) & (kc < sc + win_c)
    select = (kc - qc + win_c - 1 == np.arange(2 * win_c - 1)[:, None, None]) & valid
    cols = jnp.einsum("lhrd,dqk->lhrqk", rpb.astype(F32), jnp.asarray(select, F32), precision=lax.Precision.HIGHEST)
    cols = jnp.where(valid, cols, NEG)
    return jnp.concatenate([cols[:, :, :-1], cols[:, :, 1:]], axis=-1)


def kernel(x_prompt, x_sample, cache_k, cache_v, c, c_ctx, w_ada, b_ada, g_pre, g_post, w_in, a_conv_w, a_conv_b,
           a_ln_g, a_ln_b, b_ln_g, b_ln_b, b_ws, b_bias, c_conv_w, d_rpb, w_out):
    batch, seq, d = x_prompt.shape
    dec_batch, dec_seq, _ = x_sample.shape
    depth = w_in.shape[0]
    wb = d // 4
    nh, hd = cache_k.shape[3], cache_k.shape[4]
    past = cache_k.shape[2]
    assert nh * hd == wb and w_in.shape[2] == N_IN_SLICES * wb and dec_seq % GRID_W == 0

    n_rows = -(-(1 + dec_batch) // 8) * 8
    cond = jnp.zeros((n_rows, d), F32).at[0].set(c_ctx).at[1:1 + dec_batch].set(c)
    mod = _ada_modulation(cond, w_ada, b_ada).reshape(depth, n_rows, 3, 1, d)

    vec = lambda a: a.reshape(depth, 1, a.shape[-1])
    g_pre, g_post, a_conv_b, a_ln_g, a_ln_b, b_ln_g, b_ln_b = map(
        vec, (g_pre, g_post, a_conv_b, a_ln_g, a_ln_b, b_ln_g, b_ln_b))
    ws_b = b_ws.astype(BF16)
    gh = wb // b_ws.shape[1]
    bias_full = jnp.repeat(jnp.swapaxes(b_bias, 1, 2), gh, axis=2)
    bias_tab = _neighbourhood_bias(d_rpb)
    ck = cache_k.reshape(dec_batch, depth, past, wb)
    cv = cache_v.reshape(dec_batch, depth, past, wb)

    xs = [x_prompt.reshape(batch * seq, d), x_sample.reshape(dec_batch * dec_seq, d)]
    seqs = [seq, dec_seq]
    row0 = [0, 1]
    mod_seq = [batch * seq, dec_seq]
    p_dtype = [F32, BF16]
    hn = [_prenorm(xs[i], g_pre, mod, 0, row0[i], mod_seq[i]) for i in range(2)]
    new_k = jnp.zeros((batch, depth, seq, wb), F32)
    new_v = jnp.zeros((batch, depth, seq, wb), F32)
    w_in_l, w_out_l = w_in[0].astype(BF16), w_out[0].astype(BF16)
    lat_rows = xs[1].shape[0]
    lat_row_tiles = lat_rows // _proj_tiles(lat_rows, d)[0]
    for l in range(depth):
        has_next = l + 1 < depth

        def branches(i, p):
            return _seq_branches(p, seqs[i], l, a_conv_w, a_conv_b, a_ln_g, a_ln_b, b_ln_g, b_ln_b, ws_b, bias_full,
                                 c_conv_w)

        p = _in_proj(hn[0], w_in_l, p_dtype[0])[0]
        yd, new_k, new_v = _branch_d_context(p, seq, nh, l, new_k, new_v)
        y = _out_proj((*branches(0, p), yd), w_out_l)[0]
        xs[0], hn[0] = _post(xs[0], y, g_post, g_pre, mod, l, row0[0], mod_seq[0], has_next)

        rider = lambda w: (_weight_round_rider(w, l + 1, lat_row_tiles, _proj_tiles(lat_rows, w.shape[2])[1]),) \
            if has_next else ()
        p, *w_in_next = _in_proj(hn[1], w_in_l, p_dtype[1], rider(w_in))
        yd = _branch_d_latent(p, dec_seq, ck, cv, bias_tab, l)
        y, *w_out_next = _out_proj((*branches(1, p), yd), w_out_l, rider(w_out))
        xs[1], hn[1] = _post(xs[1], y, g_post, g_pre, mod, l, row0[1], mod_seq[1], has_next)
        if has_next:
            w_in_l, w_out_l = w_in_next[0], w_out_next[0]
    return (xs[0].reshape(batch, seq, d), xs[1].reshape(dec_batch, dec_seq, d),
            new_k.reshape(batch, depth, seq, nh, hd), new_v.reshape(batch, depth, seq, nh, hd))
```

```python
import collections
import functools

import jax
import jax.numpy as jnp
import numpy as np
from jax import lax
from jax.experimental import pallas as pl
from jax.experimental.pallas import tpu as pltpu

GRID_W = 64
EPS = 1e-6
NEG = -1e30
N_IN_SLICES = 14
HALO = 16
VMEM_LIMIT_BYTES = 56 * 1024 * 1024
MM_TILE = 1024
SEQ_TILE = 512
NORM_ROWS = 256
CONV_ROWS = 64
ATTN_HEADS_PER_STEP = 2
ATTN_ROW_GROUP = 4
LANES = 128
SUBLANES = 8

F32 = jnp.float32
BF16 = jnp.bfloat16


def _tile(n, target, unit=128):
    if n <= target:
        return n
    t = target - target % unit
    while n % t:
        t -= unit
    return t


def _params(*sem):
    return pltpu.CompilerParams(dimension_semantics=sem, vmem_limit_bytes=VMEM_LIMIT_BYTES)


def _silu(x):
    return x * jax.nn.sigmoid(x)


def _rms(x, g):
    return x * lax.rsqrt(jnp.mean(x * x, axis=-1, keepdims=True) + EPS) * g


def _ln(x, g, b):
    mu = jnp.mean(x, axis=-1, keepdims=True)
    xc = x - mu
    var = jnp.mean(xc * xc, axis=-1, keepdims=True)
    return xc * lax.rsqrt(var + EPS) * g + b


def _ada_kernel(cond_ref, w_ref, b_ref, o_ref):
    a = _silu(cond_ref[...]).astype(BF16)
    o_ref[...] = jnp.dot(a, w_ref[...].astype(BF16), preferred_element_type=F32) + b_ref[...]


def _ada_modulation(cond, w_ada, b_ada):
    depth, d, n = w_ada.shape
    r = cond.shape[0]
    tn = _tile(n, 512)
    return pl.pallas_call(
        _ada_kernel,
        grid=(depth, n // tn),
        in_specs=[pl.BlockSpec((r, d), lambda l, j: (0, 0)),
                  pl.BlockSpec((None, d, tn), lambda l, j: (l, 0, j)),
                  pl.BlockSpec((None, 1, tn), lambda l, j: (l, 0, j))],
        out_specs=pl.BlockSpec((None, r, tn), lambda l, j: (l, 0, j)),
        out_shape=jax.ShapeDtypeStruct((depth, r, n), F32),
        compiler_params=_params("arbitrary", "arbitrary"),
        name="ada_modulation",
    )(cond, w_ada, b_ada.reshape(depth, 1, n))


def _prenorm_kernel(x_ref, g_ref, mod_ref, o_ref):
    y = _rms(x_ref[...], g_ref[...])
    o_ref[...] = (y * (1.0 + mod_ref[1]) + mod_ref[0]).astype(o_ref.dtype)


def _post_kernel(x_ref, y_ref, gpost_ref, mod_ref, *rest, has_next):
    xn = x_ref[...] + mod_ref[2] * _rms(y_ref[...].astype(F32), gpost_ref[...])
    if has_next:
        gpre_ref, modn_ref, xo_ref, hn_ref = rest
        hn_ref[...] = (_rms(xn, gpre_ref[...]) * (1.0 + modn_ref[1]) + modn_ref[0]).astype(hn_ref.dtype)
    else:
        (xo_ref,) = rest
    xo_ref[...] = xn


def _row_tile(t):
    return min(NORM_ROWS, t)


def _vec_spec(layer, width):
    return pl.BlockSpec((None, 1, width), lambda *_: (layer, 0, 0))


def _mod_spec(layer, row0, tiles_per_row, d):
    return pl.BlockSpec((None, None, 3, 1, d), lambda i: (layer, row0 + i // tiles_per_row, 0, 0, 0))


Rider = collections.namedtuple("Rider", "fn in_specs args out_specs out_shape")


def _run_riders(riders, in_refs, out_refs):
    for r in riders:
        n_in, n_out = len(r.in_specs), len(r.out_specs)
        r.fn(*in_refs[:n_in], *out_refs[:n_out])
        in_refs, out_refs = in_refs[n_in:], out_refs[n_out:]


def _prenorm(x, g, mod, layer, row0, seq_len):
    t, d = x.shape
    tm = _row_tile(seq_len)
    return pl.pallas_call(
        _prenorm_kernel,
        grid=(t // tm,),
        in_specs=[pl.BlockSpec((tm, d), lambda i: (i, 0)),
                  _vec_spec(layer, d),
                  _mod_spec(layer, row0, seq_len // tm, d)],
        out_specs=pl.BlockSpec((tm, d), lambda i: (i, 0)),
        out_shape=jax.ShapeDtypeStruct((t, d), BF16),
        compiler_params=_params("arbitrary"),
        name="prenorm",
    )(x, g, mod)


def _post(x, y, g_post, g_pre, mod, layer, row0, seq_len, has_next):
    t, d = x.shape
    tm = _row_tile(seq_len)
    tpr = seq_len // tm
    row = pl.BlockSpec((tm, d), lambda i: (i, 0))
    in_specs = [row, row, _vec_spec(layer, d), _mod_spec(layer, row0, tpr, d)]
    args = [x, y, g_post, mod]
    out_specs = [row]
    out_shape = [jax.ShapeDtypeStruct((t, d), F32)]
    if has_next:
        in_specs += [_vec_spec(layer + 1, d), _mod_spec(layer + 1, row0, tpr, d)]
        args += [g_pre, mod]
        out_specs.append(row)
        out_shape.append(jax.ShapeDtypeStruct((t, d), BF16))
    out = pl.pallas_call(
        functools.partial(_post_kernel, has_next=has_next),
        grid=(t // tm,),
        in_specs=in_specs,
        out_specs=out_specs,
        out_shape=out_shape,
        compiler_params=_params("arbitrary"),
        name="post",
    )(*args)
    return (out[0], out[1]) if has_next else (out[0], None)


def _round_kernel(w_ref, o_ref):
    o_ref[...] = w_ref[...].astype(o_ref.dtype)


def _weight_round_rider(w_all, layer, grid_rows, tn):
    k = w_all.shape[1]
    rows = k // grid_rows
    assert rows * grid_rows == k and rows % (2 * SUBLANES) == 0
    return Rider(_round_kernel,
                 [pl.BlockSpec((None, rows, tn), lambda i, j: (layer, i, j))], [w_all],
                 [pl.BlockSpec((rows, tn), lambda i, j: (i, j))], [jax.ShapeDtypeStruct(w_all.shape[1:], BF16)])


def _rider_lists(riders):
    cat = lambda field: [v for r in riders for v in getattr(r, field)]
    return cat("in_specs"), cat("args"), cat("out_specs"), cat("out_shape")


def _in_proj_kernel(x_ref, w_ref, *rest, riders):
    n_in = sum(len(r.in_specs) for r in riders)
    o_ref = rest[n_in]
    o_ref[...] = jnp.dot(x_ref[...], w_ref[...], preferred_element_type=F32).astype(o_ref.dtype)
    _run_riders(riders, rest[:n_in], rest[n_in + 1:])


def _proj_tiles(t, n):
    return _tile(t, MM_TILE), _tile(n, MM_TILE)


def _in_proj(hn, w, out_dtype, riders=()):
    t, d = hn.shape
    n = w.shape[1]
    tm, tn = _proj_tiles(t, n)
    r_in, r_args, r_out, r_shape = _rider_lists(riders)
    return pl.pallas_call(
        functools.partial(_in_proj_kernel, riders=riders),
        grid=(t // tm, n // tn),
        in_specs=[pl.BlockSpec((tm, d), lambda i, j: (i, 0)), pl.BlockSpec((d, tn), lambda i, j: (0, j)), *r_in],
        out_specs=[pl.BlockSpec((tm, tn), lambda i, j: (i, j)), *r_out],
        out_shape=[jax.ShapeDtypeStruct((t, n), out_dtype), *r_shape],
        compiler_params=_params("arbitrary", "arbitrary"),
        name="in_proj",
    )(hn, w, *r_args)


def _out_proj_kernel(a_ref, b_ref, c_ref, d_ref, w_ref, *rest, riders):
    n_in = sum(len(r.in_specs) for r in riders)
    o_ref, cat_ref = rest[n_in], rest[-1]
    wb = a_ref.shape[1]

    @pl.when(pl.program_id(1) == 0)
    def _():
        for s, ref in enumerate((a_ref, b_ref, c_ref, d_ref)):
            cat_ref[:, s * wb:(s + 1) * wb] = ref[...]

    o_ref[...] = jnp.dot(cat_ref[...], w_ref[...], preferred_element_type=F32).astype(o_ref.dtype)
    _run_riders(riders, rest[:n_in], rest[n_in + 1:-1])


def _out_proj(branches, w, riders=()):
    t, wb = branches[0].shape
    k, n = w.shape
    tm, tn = _proj_tiles(t, n)
    slab = pl.BlockSpec((tm, wb), lambda i, j: (i, 0))
    r_in, r_args, r_out, r_shape = _rider_lists(riders)
    return pl.pallas_call(
        functools.partial(_out_proj_kernel, riders=riders),
        grid=(t // tm, n // tn),
        in_specs=[slab, slab, slab, slab, pl.BlockSpec((k, tn), lambda i, j: (0, j)), *r_in],
        out_specs=[pl.BlockSpec((tm, tn), lambda i, j: (i, j)), *r_out],
        out_shape=[jax.ShapeDtypeStruct((t, n), BF16), *r_shape],
        scratch_shapes=[pltpu.VMEM((tm, k), BF16)],
        compiler_params=_params("arbitrary", "arbitrary"),
        name="out_proj",
    )(*branches, w, *r_args)


def _seq_specs(tl, wb, total_rows, cols):
    r = tl // HALO
    last = total_rows // HALO - 1
    main = pl.BlockSpec((tl, wb), lambda i: (i, cols))
    prev = pl.BlockSpec((HALO, wb), lambda i: (jnp.maximum(i * r - 1, 0), cols))
    nxt = pl.BlockSpec((HALO, wb), lambda i: (jnp.minimum((i + 1) * r, last), cols))
    return main, prev, nxt


def _fill_halo(z_ref, fn, main, prev, nxt, tl, tiles_per_seq):
    i = pl.program_id(0) % tiles_per_seq
    z_ref[0:HALO, :] = jnp.where(i != 0, fn(*[r[...].astype(F32) for r in prev]), 0.0)
    z_ref[HALO:HALO + tl, :] = fn(*[r[...].astype(F32) for r in main])
    z_ref[HALO + tl:, :] = jnp.where(i != tiles_per_seq - 1, fn(*[r[...].astype(F32) for r in nxt]), 0.0)


def _conv_block(z_ref, cw_ref, r0, rc, lanes):
    taps = cw_ref.shape[0]
    first = HALO - (taps - 1) // 2
    win = z_ref[pl.ds(r0, rc + 2 * HALO), lanes]
    acc = None
    for s in range(SUBLANES):
        offs = [o for o in range(first, first + taps) if o % SUBLANES == s]
        if offs:
            rolled = pltpu.roll(win, win.shape[0] - s, 0) if s else win
            for o in offs:
                term = cw_ref[o - first:o - first + 1, lanes] * rolled[o - s:o - s + rc]
                acc = term if acc is None else acc + term
    return acc


def _conformer_kernel(val, val_p, val_n, glu, glu_p, glu_n, gate_ref, cw_ref, cb_ref, g_ref, b_ref, o_ref, z_ref,
                      y_ref, *, tl, tiles_per_seq):
    _fill_halo(z_ref, lambda v, g: v * jax.nn.sigmoid(g), (val, glu), (val_p, glu_p), (val_n, glu_n), tl, tiles_per_seq)
    wb = o_ref.shape[1]
    rc = min(CONV_ROWS, tl)

    def chunk(ci, carry):
        r0 = pl.multiple_of(ci * rc, rc)
        for cb in range(wb // LANES):
            lanes = slice(cb * LANES, (cb + 1) * LANES)
            y_ref[pl.ds(r0, rc), lanes] = _conv_block(z_ref, cw_ref, r0, rc, lanes) + cb_ref[:, lanes]
        y = _silu(_ln(y_ref[pl.ds(r0, rc), :], g_ref[...], b_ref[...]))
        o_ref[pl.ds(r0, rc), :] = (y * _silu(gate_ref[pl.ds(r0, rc), :].astype(F32))).astype(o_ref.dtype)
        return carry

    lax.fori_loop(0, tl // rc, chunk, 0)


def _gmlp_kernel(u_ref, v_ref, gate_ref, g_ref, b_ref, ws_ref, bias_ref, o_ref, *, tl):
    nh, chunk = ws_ref.shape[0], ws_ref.shape[1]
    hd = o_ref.shape[1] // nh
    vn = _ln(v_ref[...].astype(F32), g_ref[...], b_ref[...]).astype(BF16)
    for c in range(tl // chunk):
        rows = slice(c * chunk, (c + 1) * chunk)
        for h in range(nh):
            cols = slice(h * hd, (h + 1) * hd)
            s = jnp.dot(ws_ref[h], vn[rows, cols], preferred_element_type=F32) + bias_ref[:, cols]
            y = u_ref[rows, cols].astype(F32) * s * _silu(gate_ref[rows, cols].astype(F32))
            o_ref[rows, cols] = y.astype(o_ref.dtype)


def _short_conv_kernel(cc, cc_p, cc_n, cx, cx_p, cx_n, cb_ref, gate_ref, cw_ref, o_ref, z_ref, *, tl, tiles_per_seq):
    _fill_halo(z_ref, lambda a, b: a * b, (cc, cx), (cc_p, cx_p), (cc_n, cx_n), tl, tiles_per_seq)
    wb = o_ref.shape[1]
    rc = min(CONV_ROWS, tl)

    def chunk(ci, carry):
        r0 = pl.multiple_of(ci * rc, rc)
        for blk in range(wb // LANES):
            lanes = slice(blk * LANES, (blk + 1) * LANES)
            y = cb_ref[pl.ds(r0, rc), lanes].astype(F32) * _conv_block(z_ref, cw_ref, r0, rc, lanes)
            o_ref[pl.ds(r0, rc), lanes] = (y * _silu(gate_ref[pl.ds(r0, rc), lanes].astype(F32))).astype(o_ref.dtype)
        return carry

    lax.fori_loop(0, tl // rc, chunk, 0)


_N_CONFORMER_IN, _N_GMLP_IN, _N_SHORT_CONV_IN = 11, 7, 9


def _seq_branches_kernel(*refs, tl, tiles_per_seq):
    a_in, refs = refs[:_N_CONFORMER_IN], refs[_N_CONFORMER_IN:]
    b_in, refs = refs[:_N_GMLP_IN], refs[_N_GMLP_IN:]
    c_in, (oa_ref, ob_ref, oc_ref, z_ref, y_ref) = refs[:_N_SHORT_CONV_IN], refs[_N_SHORT_CONV_IN:]
    _conformer_kernel(*a_in, oa_ref, z_ref, y_ref, tl=tl, tiles_per_seq=tiles_per_seq)
    _gmlp_kernel(*b_in, ob_ref, tl=tl)
    _short_conv_kernel(*c_in, oc_ref, z_ref, tl=tl, tiles_per_seq=tiles_per_seq)


def _seq_branches(p, seq_len, layer, a_conv_w, a_conv_b, a_ln_g, a_ln_b, b_ln_g, b_ln_b, ws, bias_full, c_conv_w):
    t = p.shape[0]
    wb = p.shape[1] // N_IN_SLICES
    tl = _tile(seq_len, SEQ_TILE)
    nh, chunk = ws.shape[1], ws.shape[2]
    vec = _vec_spec(layer, wb)
    tile = lambda s: pl.BlockSpec((tl, wb), lambda i: (i, s))
    taps = lambda w: pl.BlockSpec((None, w.shape[1], wb), lambda i: (layer, 0, 0))
    a_specs = [*_seq_specs(tl, wb, t, 0), *_seq_specs(tl, wb, t, 1), tile(2), taps(a_conv_w), vec, vec, vec]
    b_specs = [tile(3), tile(4), tile(5), vec, vec,
               pl.BlockSpec((None, nh, chunk, chunk), lambda i: (layer, 0, 0, 0)),
               pl.BlockSpec((None, chunk, wb), lambda i: (layer, 0, 0))]
    c_specs = [*_seq_specs(tl, wb, t, 7), *_seq_specs(tl, wb, t, 8), tile(6), tile(9), taps(c_conv_w)]
    assert (len(a_specs), len(b_specs), len(c_specs)) == (_N_CONFORMER_IN, _N_GMLP_IN, _N_SHORT_CONV_IN)
    slab = pl.BlockSpec((tl, wb), lambda i: (i, 0))
    return pl.pallas_call(
        functools.partial(_seq_branches_kernel, tl=tl, tiles_per_seq=seq_len // tl),
        grid=(t // tl,),
        in_specs=[*a_specs, *b_specs, *c_specs],
        out_specs=[slab, slab, slab],
        out_shape=[jax.ShapeDtypeStruct((t, wb), BF16)] * 3,
        scratch_shapes=[pltpu.VMEM((tl + 2 * HALO, wb), F32), pltpu.VMEM((tl, wb), F32)],
        compiler_params=_params("arbitrary"),
        name="branches_abc",
    )(*[p] * 7, a_conv_w, a_conv_b, a_ln_g, a_ln_b, *[p] * 3, b_ln_g, b_ln_b, ws, bias_full, *[p] * 8, c_conv_w)


_NT = (((1,), (1,)), ((), ()))


def _ctx_attn_kernel(q_ref, k_ref, v_ref, gate_ref, kall_ref, vall_ref, o_ref, knew_ref, vnew_ref, *, nh):
    del kall_ref, vall_ref
    knew_ref[...] = k_ref[...].astype(knew_ref.dtype)
    vnew_ref[...] = v_ref[...].astype(vnew_ref.dtype)
    hd = o_ref.shape[1] // nh
    scale = hd ** -0.5
    for h in range(nh):
        cols = slice(h * hd, (h + 1) * hd)
        q = q_ref[:, cols].astype(BF16)
        s = lax.dot_general(q, k_ref[:, cols].astype(BF16), _NT, preferred_element_type=F32) * scale
        e = jnp.exp(s - jnp.max(s, axis=-1, keepdims=True))
        o = jnp.dot(e.astype(BF16), v_ref[:, cols].astype(BF16), preferred_element_type=F32)
        o = o / jnp.sum(e, axis=-1, keepdims=True)
        o_ref[:, cols] = (o * _silu(gate_ref[:, cols].astype(F32))).astype(o_ref.dtype)


def _branch_d_context(p, seq_len, nh, layer, new_k, new_v):
    t = p.shape[0]
    wb = p.shape[1] // N_IN_SLICES
    kv_out = pl.BlockSpec((None, None, seq_len, wb), lambda b: (b, layer, 0, 0))
    any_space = pl.BlockSpec(memory_space=pl.ANY)
    return pl.pallas_call(
        functools.partial(_ctx_attn_kernel, nh=nh),
        grid=(t // seq_len,),
        in_specs=[*[pl.BlockSpec((seq_len, wb), lambda b, s=s: (b, s)) for s in (10, 11, 12, 13)], any_space, any_space],
        out_specs=[pl.BlockSpec((seq_len, wb), lambda b: (b, 0)), kv_out, kv_out],
        out_shape=[jax.ShapeDtypeStruct((t, wb), BF16), jax.ShapeDtypeStruct(new_k.shape, new_k.dtype),
                   jax.ShapeDtypeStruct(new_v.shape, new_v.dtype)],
        input_output_aliases={4: 1, 5: 2},
        compiler_params=_params("arbitrary"),
        name="branch_d_context_attention",
    )(p, p, p, p, new_k, new_v)


def _nbr_attn_kernel(q_ref, k_ref, v_ref, gate_ref, ck_ref, cv_ref, bias_ref, o_ref, oc_ref, mc_ref, lc_ref,
                     *, rows, kr, hd):
    for h in range(o_ref.shape[1] // hd):
        cols = slice(h * hd, (h + 1) * hd)
        _nbr_attn_head(q_ref, k_ref, v_ref, gate_ref, ck_ref, cv_ref, bias_ref.at[h], o_ref, oc_ref, mc_ref, lc_ref,
                       cols, rows, kr)


def _nbr_attn_head(q_ref, k_ref, v_ref, gate_ref, ck_ref, cv_ref, bias_ref, o_ref, oc_ref, mc_ref, lc_ref,
                   cols, rows, kr):
    hd = cols.stop - cols.start
    seq_len = q_ref.shape[0]
    win_r = bias_ref.shape[0] // 2 + 1
    scale = hd ** -0.5
    ck = ck_ref[:, cols].astype(BF16)
    cv = cv_ref[:, cols].astype(BF16)

    cq = _tile(seq_len, 512)

    def ctx_chunk(ci, carry):
        r0 = pl.multiple_of(ci * cq, cq)
        s = lax.dot_general(q_ref[pl.ds(r0, cq), cols].astype(BF16), ck, _NT, preferred_element_type=F32) * scale
        m = jnp.max(s, axis=-1, keepdims=True)
        e = jnp.exp(s - m)
        mc_ref[pl.ds(r0, cq), :] = m
        lc_ref[pl.ds(r0, cq), :] = jnp.sum(e, axis=-1, keepdims=True)
        oc_ref[pl.ds(r0, cq), :] = jnp.dot(e.astype(BF16), cv, preferred_element_type=F32)
        return carry

    lax.fori_loop(0, seq_len // cq, ctx_chunk, 0, unroll=True)

    group = ATTN_ROW_GROUP if rows % ATTN_ROW_GROUP == 0 else 1

    def window_starts(gi):
        rs = [gi * group + i for i in range(group)]
        starts = [jnp.clip(r - kr // 2, 0, rows - kr) for r in rs]
        return rs, starts, [pl.multiple_of(s * GRID_W, GRID_W) for s in starts]

    def scores_stage(gi):
        rs, starts, k0s = window_starts(gi)
        out = []
        for r, s, k0 in zip(rs, starts, k0s):
            q0 = pl.multiple_of(r * GRID_W, GRID_W)
            base = win_r - 1 - (r - s)
            bias = jnp.concatenate([bias_ref[base + 2 * t] for t in range(kr // 2)], axis=1)
            out.append(lax.dot_general(q_ref[pl.ds(q0, GRID_W), cols].astype(BF16),
                                       k_ref[pl.ds(k0, kr * GRID_W), cols].astype(BF16), _NT,
                                       preferred_element_type=F32) * scale + bias)
        return tuple(out)

    def softmax_stage(scores):
        out = []
        for sc in scores:
            m = jnp.max(sc, axis=-1, keepdims=True)
            e = jnp.exp(sc - m)
            out.append((e.astype(BF16), m, jnp.sum(e, axis=-1, keepdims=True)))
        return tuple(out)

    def values_stage(gi, stats):
        rs, _, k0s = window_starts(gi)
        for r, k0, (e, m_l, l_l) in zip(rs, k0s, stats):
            q0 = pl.multiple_of(r * GRID_W, GRID_W)
            o_l = jnp.dot(e, v_ref[pl.ds(k0, kr * GRID_W), cols].astype(BF16), preferred_element_type=F32)
            m_c = mc_ref[pl.ds(q0, GRID_W), :]
            m = jnp.maximum(m_l, m_c)
            a_l = jnp.exp(m_l - m)
            a_c = jnp.exp(m_c - m)
            o = (a_l * o_l + a_c * oc_ref[pl.ds(q0, GRID_W), :]) / (a_l * l_l + a_c * lc_ref[pl.ds(q0, GRID_W), :])
            g = gate_ref[pl.ds(q0, GRID_W), cols].astype(F32)
            o_ref[pl.ds(q0, GRID_W), cols] = (o * _silu(g)).astype(o_ref.dtype)

    def pipelined(gi, stats):
        nxt = softmax_stage(scores_stage(gi + 1))
        values_stage(gi, stats)
        return nxt

    n_groups = rows // group
    last = lax.fori_loop(0, n_groups - 1, pipelined, softmax_stage(scores_stage(jnp.int32(0))))
    values_stage(jnp.int32(n_groups - 1), last)


def _branch_d_latent(p, seq_len, cache_k, cache_v, bias_tab, layer):
    t = p.shape[0]
    wb = p.shape[1] // N_IN_SLICES
    nh, n_off = bias_tab.shape[1], bias_tab.shape[2]
    hd = wb // nh
    past = cache_k.shape[2]
    rows = seq_len // GRID_W
    kr = min(n_off // 2 + 1, rows)
    assert kr % 2 == 0
    hp = ATTN_HEADS_PER_STEP if nh % ATTN_HEADS_PER_STEP == 0 else 1
    steps = nh // hp
    col = lambda s: pl.BlockSpec((seq_len, hp * hd), lambda b, h, s=s: (b, s * steps + h))
    ctx = pl.BlockSpec((None, None, past, hp * hd), lambda b, h: (b, layer, 0, h))
    return pl.pallas_call(
        functools.partial(_nbr_attn_kernel, rows=rows, kr=kr, hd=hd),
        grid=(t // seq_len, steps),
        in_specs=[col(10), col(11), col(12), col(13), ctx, ctx,
                  pl.BlockSpec((None, hp, n_off, GRID_W, 2 * GRID_W), lambda b, h: (layer, h, 0, 0, 0))],
        out_specs=pl.BlockSpec((seq_len, hp * hd), lambda b, h: (b, h)),
        out_shape=jax.ShapeDtypeStruct((t, wb), BF16),
        scratch_shapes=[pltpu.VMEM((seq_len, hd), F32), pltpu.VMEM((seq_len, 1), F32), pltpu.VMEM((seq_len, 1), F32)],
        compiler_params=_params("arbitrary", "arbitrary"),
        name="branch_d_neighbourhood_attention",
    )(p, p, p, p, cache_k, cache_v, bias_tab)


def _neighbourhood_bias(rpb):
    win_c = (rpb.shape[3] + 1) // 2
    qc = np.arange(GRID_W)[:, None]
    kc = np.arange(GRID_W)[None, :]
    sc = np.clip(qc - win_c // 2, 0, GRID_W - win_c)
    valid = (kc >= sc) & (kc < sc + win_c)
    select = (kc - qc + win_c - 1 == np.arange(2 * win_c - 1)[:, None, None]) & valid
    cols = jnp.einsum("lhrd,dqk->lhrqk", rpb.astype(F32), jnp.asarray(select, F32), precision=lax.Precision.HIGHEST)
    cols = jnp.where(valid, cols, NEG)
    return jnp.concatenate([cols[:, :, :-1], cols[:, :, 1:]], axis=-1)


def kernel(x_prompt, x_sample, cache_k, cache_v, c, c_ctx, w_ada, b_ada, g_pre, g_post, w_in, a_conv_w, a_conv_b,
           a_ln_g, a_ln_b, b_ln_g, b_ln_b, b_ws, b_bias, c_conv_w, d_rpb, w_out):
    batch, seq, d = x_prompt.shape
    dec_batch, dec_seq, _ = x_sample.shape
    depth = w_in.shape[0]
    wb = d // 4
    nh, hd = cache_k.shape[3], cache_k.shape[4]
    past = cache_k.shape[2]
    assert nh * hd == wb and w_in.shape[2] == N_IN_SLICES * wb and dec_seq % GRID_W == 0

    n_rows = -(-(1 + dec_batch) // 8) * 8
    cond = jnp.zeros((n_rows, d), F32).at[0].set(c_ctx).at[1:1 + dec_batch].set(c)
    mod = _ada_modulation(cond, w_ada, b_ada).reshape(depth, n_rows, 3, 1, d)

    vec = lambda a: a.reshape(depth, 1, a.shape[-1])
    g_pre, g_post, a_conv_b, a_ln_g, a_ln_b, b_ln_g, b_ln_b = map(
        vec, (g_pre, g_post, a_conv_b, a_ln_g, a_ln_b, b_ln_g, b_ln_b))
    ws_b = b_ws.astype(BF16)
    gh = wb // b_ws.shape[1]
    bias_full = jnp.repeat(jnp.swapaxes(b_bias, 1, 2), gh, axis=2)
    bias_tab = _neighbourhood_bias(d_rpb)
    ck = cache_k.reshape(dec_batch, depth, past, wb)
    cv = cache_v.reshape(dec_batch, depth, past, wb)

    xs = [x_prompt.reshape(batch * seq, d), x_sample.reshape(dec_batch * dec_seq, d)]
    seqs = [seq, dec_seq]
    row0 = [0, 1]
    mod_seq = [batch * seq, dec_seq]
    p_dtype = [F32, BF16]
    hn = [_prenorm(xs[i], g_pre, mod, 0, row0[i], mod_seq[i]) for i in range(2)]
    new_k = jnp.zeros((batch, depth, seq, wb), F32)
    new_v = jnp.zeros((batch, depth, seq, wb), F32)
    w_in_l, w_out_l = w_in[0].astype(BF16), w_out[0].astype(BF16)
    lat_rows = xs[1].shape[0]
    lat_row_tiles = lat_rows // _proj_tiles(lat_rows, d)[0]
    for l in range(depth):
        has_next = l + 1 < depth

        def branches(i, p):
            return _seq_branches(p, seqs[i], l, a_conv_w, a_conv_b, a_ln_g, a_ln_b, b_ln_g, b_ln_b, ws_b, bias_full,
                                 c_conv_w)

        p = _in_proj(hn[0], w_in_l, p_dtype[0])[0]
        yd, new_k, new_v = _branch_d_context(p, seq, nh, l, new_k, new_v)
        y = _out_proj((*branches(0, p), yd), w_out_l)[0]
        xs[0], hn[0] = _post(xs[0], y, g_post, g_pre, mod, l, row0[0], mod_seq[0], has_next)

        rider = lambda w: (_weight_round_rider(w, l + 1, lat_row_tiles, _proj_tiles(lat_rows, w.shape[2])[1]),) \
            if has_next else ()
        p, *w_in_next = _in_proj(hn[1], w_in_l, p_dtype[1], rider(w_in))
        yd = _branch_d_latent(p, dec_seq, ck, cv, bias_tab, l)
        y, *w_out_next = _out_proj((*branches(1, p), yd), w_out_l, rider(w_out))
        xs[1], hn[1] = _post(xs[1], y, g_post, g_pre, mod, l, row0[1], mod_seq[1], has_next)
        if has_next:
            w_in_l, w_out_l = w_in_next[0], w_out_next[0]
    return (xs[0].reshape(batch, seq, d), xs[1].reshape(dec_batch, dec_seq, d),
            new_k.reshape(batch, depth, seq, nh, hd), new_v.reshape(batch, depth, seq, nh, hd))
```

```python
import collections
import functools

import jax
import jax.numpy as jnp
import numpy as np
from jax import lax
from jax.experimental import pallas as pl
from jax.experimental.pallas import tpu as pltpu

GRID_W = 64
EPS = 1e-6
NEG = -1e30
LOG2E = 1.4426950408889634
N_IN_SLICES = 14
HALO = 16
VMEM_LIMIT_BYTES = 56 * 1024 * 1024
MM_TILE = 1024
SEQ_TILE = 512
NORM_ROWS = 256
CONV_ROWS = 64
ATTN_HEADS_PER_STEP = 2
ATTN_ROW_GROUP = 4
LANES = 128
SUBLANES = 8

F32 = jnp.float32
BF16 = jnp.bfloat16


def _tile(n, target, unit=128):
    if n <= target:
        return n
    t = target - target % unit
    while n % t:
        t -= unit
    return t


def _params(*sem):
    return pltpu.CompilerParams(dimension_semantics=sem, vmem_limit_bytes=VMEM_LIMIT_BYTES)


def _silu(x):
    return x * jax.nn.sigmoid(x)


def _rms(x, g):
    return x * lax.rsqrt(jnp.mean(x * x, axis=-1, keepdims=True) + EPS) * g


def _ln(x, g, b):
    mu = jnp.mean(x, axis=-1, keepdims=True)
    xc = x - mu
    var = jnp.mean(xc * xc, axis=-1, keepdims=True)
    return xc * lax.rsqrt(var + EPS) * g + b


def _ada_kernel(cond_ref, w_ref, b_ref, o_ref):
    a = _silu(cond_ref[...]).astype(BF16)
    o_ref[...] = jnp.dot(a, w_ref[...].astype(BF16), preferred_element_type=F32) + b_ref[...]


def _ada_modulation(cond, w_ada, b_ada):
    depth, d, n = w_ada.shape
    r = cond.shape[0]
    tn = _tile(n, 512)
    return pl.pallas_call(
        _ada_kernel,
        grid=(depth, n // tn),
        in_specs=[pl.BlockSpec((r, d), lambda l, j: (0, 0)),
                  pl.BlockSpec((None, d, tn), lambda l, j: (l, 0, j)),
                  pl.BlockSpec((None, 1, tn), lambda l, j: (l, 0, j))],
        out_specs=pl.BlockSpec((None, r, tn), lambda l, j: (l, 0, j)),
        out_shape=jax.ShapeDtypeStruct((depth, r, n), F32),
        compiler_params=_params("arbitrary", "arbitrary"),
        name="ada_modulation",
    )(cond, w_ada, b_ada.reshape(depth, 1, n))


def _prenorm_kernel(x_ref, g_ref, mod_ref, o_ref):
    y = _rms(x_ref[...], g_ref[...])
    o_ref[...] = (y * (1.0 + mod_ref[1]) + mod_ref[0]).astype(o_ref.dtype)


def _post_kernel(x_ref, y_ref, gpost_ref, mod_ref, *rest, has_next):
    xn = x_ref[...] + mod_ref[2] * _rms(y_ref[...].astype(F32), gpost_ref[...])
    if has_next:
        gpre_ref, modn_ref, xo_ref, hn_ref = rest
        hn_ref[...] = (_rms(xn, gpre_ref[...]) * (1.0 + modn_ref[1]) + modn_ref[0]).astype(hn_ref.dtype)
    else:
        (xo_ref,) = rest
    xo_ref[...] = xn


def _row_tile(t):
    return min(NORM_ROWS, t)


def _vec_spec(layer, width):
    return pl.BlockSpec((None, 1, width), lambda *_: (layer, 0, 0))


def _mod_spec(layer, row0, tiles_per_row, d):
    return pl.BlockSpec((None, None, 3, 1, d), lambda i: (layer, row0 + i // tiles_per_row, 0, 0, 0))


Rider = collections.namedtuple("Rider", "fn in_specs args out_specs out_shape")


def _run_riders(riders, in_refs, out_refs):
    for r in riders:
        n_in, n_out = len(r.in_specs), len(r.out_specs)
        r.fn(*in_refs[:n_in], *out_refs[:n_out])
        in_refs, out_refs = in_refs[n_in:], out_refs[n_out:]


def _prenorm(x, g, mod, layer, row0, seq_len):
    t, d = x.shape
    tm = _row_tile(seq_len)
    return pl.pallas_call(
        _prenorm_kernel,
        grid=(t // tm,),
        in_specs=[pl.BlockSpec((tm, d), lambda i: (i, 0)),
                  _vec_spec(layer, d),
                  _mod_spec(layer, row0, seq_len // tm, d)],
        out_specs=pl.BlockSpec((tm, d), lambda i: (i, 0)),
        out_shape=jax.ShapeDtypeStruct((t, d), BF16),
        compiler_params=_params("arbitrary"),
        name="prenorm",
    )(x, g, mod)


def _post(x, y, g_post, g_pre, mod, layer, row0, seq_len, has_next):
    t, d = x.shape
    tm = _row_tile(seq_len)
    tpr = seq_len // tm
    row = pl.BlockSpec((tm, d), lambda i: (i, 0))
    in_specs = [row, row, _vec_spec(layer, d), _mod_spec(layer, row0, tpr, d)]
    args = [x, y, g_post, mod]
    out_specs = [row]
    out_shape = [jax.ShapeDtypeStruct((t, d), F32)]
    if has_next:
        in_specs += [_vec_spec(layer + 1, d), _mod_spec(layer + 1, row0, tpr, d)]
        args += [g_pre, mod]
        out_specs.append(row)
        out_shape.append(jax.ShapeDtypeStruct((t, d), BF16))
    out = pl.pallas_call(
        functools.partial(_post_kernel, has_next=has_next),
        grid=(t // tm,),
        in_specs=in_specs,
        out_specs=out_specs,
        out_shape=out_shape,
        compiler_params=_params("arbitrary"),
        name="post",
    )(*args)
    return (out[0], out[1]) if has_next else (out[0], None)


def _round_kernel(w_ref, o_ref):
    o_ref[...] = w_ref[...].astype(o_ref.dtype)


def _weight_round_rider(w_all, layer, grid_rows, tn):
    k = w_all.shape[1]
    rows = k // grid_rows
    assert rows * grid_rows == k and rows % (2 * SUBLANES) == 0
    return Rider(_round_kernel,
                 [pl.BlockSpec((None, rows, tn), lambda i, j: (layer, i, j))], [w_all],
                 [pl.BlockSpec((rows, tn), lambda i, j: (i, j))], [jax.ShapeDtypeStruct(w_all.shape[1:], BF16)])


def _rider_lists(riders):
    cat = lambda field: [v for r in riders for v in getattr(r, field)]
    return cat("in_specs"), cat("args"), cat("out_specs"), cat("out_shape")


def _in_proj_kernel(x_ref, w_ref, *rest, riders):
    n_in = sum(len(r.in_specs) for r in riders)
    o_ref = rest[n_in]
    o_ref[...] = jnp.dot(x_ref[...], w_ref[...], preferred_element_type=F32).astype(o_ref.dtype)
    _run_riders(riders, rest[:n_in], rest[n_in + 1:])


def _proj_tiles(t, n):
    return _tile(t, MM_TILE), _tile(n, MM_TILE)


def _in_proj(hn, w, out_dtype, riders=()):
    t, d = hn.shape
    n = w.shape[1]
    tm, tn = _proj_tiles(t, n)
    r_in, r_args, r_out, r_shape = _rider_lists(riders)
    return pl.pallas_call(
        functools.partial(_in_proj_kernel, riders=riders),
        grid=(t // tm, n // tn),
        in_specs=[pl.BlockSpec((tm, d), lambda i, j: (i, 0)), pl.BlockSpec((d, tn), lambda i, j: (0, j)), *r_in],
        out_specs=[pl.BlockSpec((tm, tn), lambda i, j: (i, j)), *r_out],
        out_shape=[jax.ShapeDtypeStruct((t, n), out_dtype), *r_shape],
        compiler_params=_params("arbitrary", "arbitrary"),
        name="in_proj",
    )(hn, w, *r_args)


def _out_proj_kernel(a_ref, b_ref, c_ref, d_ref, w_ref, *rest, riders):
    n_in = sum(len(r.in_specs) for r in riders)
    o_ref, cat_ref = rest[n_in], rest[-1]
    wb = a_ref.shape[1]

    @pl.when(pl.program_id(1) == 0)
    def _():
        for s, ref in enumerate((a_ref, b_ref, c_ref, d_ref)):
            cat_ref[:, s * wb:(s + 1) * wb] = ref[...]

    o_ref[...] = jnp.dot(cat_ref[...], w_ref[...], preferred_element_type=F32).astype(o_ref.dtype)
    _run_riders(riders, rest[:n_in], rest[n_in + 1:-1])


def _out_proj(branches, w, riders=()):
    t, wb = branches[0].shape
    k, n = w.shape
    tm, tn = _proj_tiles(t, n)
    slab = pl.BlockSpec((tm, wb), lambda i, j: (i, 0))
    r_in, r_args, r_out, r_shape = _rider_lists(riders)
    return pl.pallas_call(
        functools.partial(_out_proj_kernel, riders=riders),
        grid=(t // tm, n // tn),
        in_specs=[slab, slab, slab, slab, pl.BlockSpec((k, tn), lambda i, j: (0, j)), *r_in],
        out_specs=[pl.BlockSpec((tm, tn), lambda i, j: (i, j)), *r_out],
        out_shape=[jax.ShapeDtypeStruct((t, n), BF16), *r_shape],
        scratch_shapes=[pltpu.VMEM((tm, k), BF16)],
        compiler_params=_params("arbitrary", "arbitrary"),
        name="out_proj",
    )(*branches, w, *r_args)


def _seq_specs(tl, wb, total_rows, cols):
    r = tl // HALO
    last = total_rows // HALO - 1
    main = pl.BlockSpec((tl, wb), lambda i: (i, cols))
    prev = pl.BlockSpec((HALO, wb), lambda i: (jnp.maximum(i * r - 1, 0), cols))
    nxt = pl.BlockSpec((HALO, wb), lambda i: (jnp.minimum((i + 1) * r, last), cols))
    return main, prev, nxt


def _fill_halo(z_ref, fn, main, prev, nxt, tl, tiles_per_seq):
    i = pl.program_id(0) % tiles_per_seq
    z_ref[0:HALO, :] = jnp.where(i != 0, fn(*[r[...].astype(F32) for r in prev]), 0.0)
    z_ref[HALO:HALO + tl, :] = fn(*[r[...].astype(F32) for r in main])
    z_ref[HALO + tl:, :] = jnp.where(i != tiles_per_seq - 1, fn(*[r[...].astype(F32) for r in nxt]), 0.0)


def _conv_block(z_ref, cw_ref, r0, rc, lanes):
    taps = cw_ref.shape[0]
    first = HALO - (taps - 1) // 2
    win = z_ref[pl.ds(r0, rc + 2 * HALO), lanes]
    acc = None
    for s in range(SUBLANES):
        offs = [o for o in range(first, first + taps) if o % SUBLANES == s]
        if offs:
            rolled = pltpu.roll(win, win.shape[0] - s, 0) if s else win
            for o in offs:
                term = cw_ref[o - first:o - first + 1, lanes] * rolled[o - s:o - s + rc]
                acc = term if acc is None else acc + term
    return acc


def _conformer_kernel(val, val_p, val_n, glu, glu_p, glu_n, gate_ref, cw_ref, cb_ref, g_ref, b_ref, o_ref, z_ref,
                      y_ref, *, tl, tiles_per_seq):
    _fill_halo(z_ref, lambda v, g: v * jax.nn.sigmoid(g), (val, glu), (val_p, glu_p), (val_n, glu_n), tl, tiles_per_seq)
    wb = o_ref.shape[1]
    rc = min(CONV_ROWS, tl)

    def chunk(ci, carry):
        r0 = pl.multiple_of(ci * rc, rc)
        for cb in range(wb // LANES):
            lanes = slice(cb * LANES, (cb + 1) * LANES)
            y_ref[pl.ds(r0, rc), lanes] = _conv_block(z_ref, cw_ref, r0, rc, lanes) + cb_ref[:, lanes]
        y = _silu(_ln(y_ref[pl.ds(r0, rc), :], g_ref[...], b_ref[...]))
        o_ref[pl.ds(r0, rc), :] = (y * _silu(gate_ref[pl.ds(r0, rc), :].astype(F32))).astype(o_ref.dtype)
        return carry

    lax.fori_loop(0, tl // rc, chunk, 0)


def _gmlp_kernel(u_ref, v_ref, gate_ref, g_ref, b_ref, ws_ref, bias_ref, o_ref, *, tl):
    nh, chunk = ws_ref.shape[0], ws_ref.shape[1]
    hd = o_ref.shape[1] // nh
    vn = _ln(v_ref[...].astype(F32), g_ref[...], b_ref[...]).astype(BF16)
    for c in range(tl // chunk):
        rows = slice(c * chunk, (c + 1) * chunk)
        for h in range(nh):
            cols = slice(h * hd, (h + 1) * hd)
            s = jnp.dot(ws_ref[h], vn[rows, cols], preferred_element_type=F32) + bias_ref[:, cols]
            y = u_ref[rows, cols].astype(F32) * s * _silu(gate_ref[rows, cols].astype(F32))
            o_ref[rows, cols] = y.astype(o_ref.dtype)


def _short_conv_kernel(cc, cc_p, cc_n, cx, cx_p, cx_n, cb_ref, gate_ref, cw_ref, o_ref, z_ref, *, tl, tiles_per_seq):
    _fill_halo(z_ref, lambda a, b: a * b, (cc, cx), (cc_p, cx_p), (cc_n, cx_n), tl, tiles_per_seq)
    wb = o_ref.shape[1]
    rc = min(CONV_ROWS, tl)

    def chunk(ci, carry):
        r0 = pl.multiple_of(ci * rc, rc)
        for blk in range(wb // LANES):
            lanes = slice(blk * LANES, (blk + 1) * LANES)
            y = cb_ref[pl.ds(r0, rc), lanes].astype(F32) * _conv_block(z_ref, cw_ref, r0, rc, lanes)
            o_ref[pl.ds(r0, rc), lanes] = (y * _silu(gate_ref[pl.ds(r0, rc), lanes].astype(F32))).astype(o_ref.dtype)
        return carry

    lax.fori_loop(0, tl // rc, chunk, 0)


_N_CONFORMER_IN, _N_GMLP_IN, _N_SHORT_CONV_IN = 11, 7, 9


def _seq_branches_kernel(*refs, tl, tiles_per_seq):
    a_in, refs = refs[:_N_CONFORMER_IN], refs[_N_CONFORMER_IN:]
    b_in, refs = refs[:_N_GMLP_IN], refs[_N_GMLP_IN:]
    c_in, (oa_ref, ob_ref, oc_ref, z_ref, y_ref) = refs[:_N_SHORT_CONV_IN], refs[_N_SHORT_CONV_IN:]
    _conformer_kernel(*a_in, oa_ref, z_ref, y_ref, tl=tl, tiles_per_seq=tiles_per_seq)
    _gmlp_kernel(*b_in, ob_ref, tl=tl)
    _short_conv_kernel(*c_in, oc_ref, z_ref, tl=tl, tiles_per_seq=tiles_per_seq)


def _seq_branches(p, seq_len, layer, a_conv_w, a_conv_b, a_ln_g, a_ln_b, b_ln_g, b_ln_b, ws, bias_full, c_conv_w):
    t = p.shape[0]
    wb = p.shape[1] // N_IN_SLICES
    tl = _tile(seq_len, SEQ_TILE)
    nh, chunk = ws.shape[1], ws.shape[2]
    vec = _vec_spec(layer, wb)
    tile = lambda s: pl.BlockSpec((tl, wb), lambda i: (i, s))
    taps = lambda w: pl.BlockSpec((None, w.shape[1], wb), lambda i: (layer, 0, 0))
    a_specs = [*_seq_specs(tl, wb, t, 0), *_seq_specs(tl, wb, t, 1), tile(2), taps(a_conv_w), vec, vec, vec]
    b_specs = [tile(3), tile(4), tile(5), vec, vec,
               pl.BlockSpec((None, nh, chunk, chunk), lambda i: (layer, 0, 0, 0)),
               pl.BlockSpec((None, chunk, wb), lambda i: (layer, 0, 0))]
    c_specs = [*_seq_specs(tl, wb, t, 7), *_seq_specs(tl, wb, t, 8), tile(6), tile(9), taps(c_conv_w)]
    assert (len(a_specs), len(b_specs), len(c_specs)) == (_N_CONFORMER_IN, _N_GMLP_IN, _N_SHORT_CONV_IN)
    slab = pl.BlockSpec((tl, wb), lambda i: (i, 0))
    return pl.pallas_call(
        functools.partial(_seq_branches_kernel, tl=tl, tiles_per_seq=seq_len // tl),
        grid=(t // tl,),
        in_specs=[*a_specs, *b_specs, *c_specs],
        out_specs=[slab, slab, slab],
        out_shape=[jax.ShapeDtypeStruct((t, wb), BF16)] * 3,
        scratch_shapes=[pltpu.VMEM((tl + 2 * HALO, wb), F32), pltpu.VMEM((tl, wb), F32)],
        compiler_params=_params("arbitrary"),
        name="branches_abc",
    )(*[p] * 7, a_conv_w, a_conv_b, a_ln_g, a_ln_b, *[p] * 3, b_ln_g, b_ln_b, ws, bias_full, *[p] * 8, c_conv_w)


_NT = (((1,), (1,)), ((), ()))


def _ctx_attn_kernel(q_ref, k_ref, v_ref, gate_ref, kall_ref, vall_ref, o_ref, knew_ref, vnew_ref, *, nh):
    del kall_ref, vall_ref
    knew_ref[...] = k_ref[...].astype(knew_ref.dtype)
    vnew_ref[...] = v_ref[...].astype(vnew_ref.dtype)
    hd = o_ref.shape[1] // nh
    scale = hd ** -0.5
    for h in range(nh):
        cols = slice(h * hd, (h + 1) * hd)
        q = q_ref[:, cols].astype(BF16)
        s = lax.dot_general(q, k_ref[:, cols].astype(BF16), _NT, preferred_element_type=F32) * scale
        e = jnp.exp(s - jnp.max(s, axis=-1, keepdims=True))
        o = jnp.dot(e.astype(BF16), v_ref[:, cols].astype(BF16), preferred_element_type=F32)
        o = o / jnp.sum(e, axis=-1, keepdims=True)
        o_ref[:, cols] = (o * _silu(gate_ref[:, cols].astype(F32))).astype(o_ref.dtype)


def _branch_d_context(p, seq_len, nh, layer, new_k, new_v):
    t = p.shape[0]
    wb = p.shape[1] // N_IN_SLICES
    kv_out = pl.BlockSpec((None, None, seq_len, wb), lambda b: (b, layer, 0, 0))
    any_space = pl.BlockSpec(memory_space=pl.ANY)
    return pl.pallas_call(
        functools.partial(_ctx_attn_kernel, nh=nh),
        grid=(t // seq_len,),
        in_specs=[*[pl.BlockSpec((seq_len, wb), lambda b, s=s: (b, s)) for s in (10, 11, 12, 13)], any_space, any_space],
        out_specs=[pl.BlockSpec((seq_len, wb), lambda b: (b, 0)), kv_out, kv_out],
        out_shape=[jax.ShapeDtypeStruct((t, wb), BF16), jax.ShapeDtypeStruct(new_k.shape, new_k.dtype),
                   jax.ShapeDtypeStruct(new_v.shape, new_v.dtype)],
        input_output_aliases={4: 1, 5: 2},
        compiler_params=_params("arbitrary"),
        name="branch_d_context_attention",
    )(p, p, p, p, new_k, new_v)


def _nbr_attn_kernel(q_ref, k_ref, v_ref, gate_ref, ck_ref, cv_ref, bias_ref, o_ref, oc_ref, mc_ref, lc_ref,
                     *, rows, kr, hd):
    for h in range(o_ref.shape[1] // hd):
        cols = slice(h * hd, (h + 1) * hd)
        _nbr_attn_head(q_ref, k_ref, v_ref, gate_ref, ck_ref, cv_ref, bias_ref.at[h], o_ref, oc_ref, mc_ref, lc_ref,
                       cols, rows, kr)


def _nbr_attn_head(q_ref, k_ref, v_ref, gate_ref, ck_ref, cv_ref, bias_ref, o_ref, oc_ref, mc_ref, lc_ref,
                   cols, rows, kr):
    hd = cols.stop - cols.start
    seq_len = q_ref.shape[0]
    win_r = bias_ref.shape[0] // 2 + 1
    scale = hd ** -0.5 * LOG2E
    ck = ck_ref[:, cols].astype(BF16)
    cv = cv_ref[:, cols].astype(BF16)

    cq = _tile(seq_len, 512)

    def ctx_chunk(ci, carry):
        r0 = pl.multiple_of(ci * cq, cq)
        s = lax.dot_general(q_ref[pl.ds(r0, cq), cols].astype(BF16), ck, _NT, preferred_element_type=F32) * scale
        m = jnp.max(s, axis=-1, keepdims=True)
        e = jnp.exp2(s - m)
        mc_ref[pl.ds(r0, cq), :] = m
        lc_ref[pl.ds(r0, cq), :] = jnp.sum(e, axis=-1, keepdims=True)
        oc_ref[pl.ds(r0, cq), :] = jnp.dot(e.astype(BF16), cv, preferred_element_type=F32)
        return carry

    lax.fori_loop(0, seq_len // cq, ctx_chunk, 0, unroll=True)

    group = ATTN_ROW_GROUP if rows % ATTN_ROW_GROUP == 0 else 1

    def window_starts(gi):
        rs = [gi * group + i for i in range(group)]
        starts = [jnp.clip(r - kr // 2, 0, rows - kr) for r in rs]
        return rs, starts, [pl.multiple_of(s * GRID_W, GRID_W) for s in starts]

    def scores_stage(gi):
        rs, starts, k0s = window_starts(gi)
        out = []
        for r, s, k0 in zip(rs, starts, k0s):
            q0 = pl.multiple_of(r * GRID_W, GRID_W)
            base = win_r - 1 - (r - s)
            bias = jnp.concatenate([bias_ref[base + 2 * t] for t in range(kr // 2)], axis=1)
            out.append(lax.dot_general(q_ref[pl.ds(q0, GRID_W), cols].astype(BF16),
                                       k_ref[pl.ds(k0, kr * GRID_W), cols].astype(BF16), _NT,
                                       preferred_element_type=F32) * scale + bias)
        return tuple(out)

    def softmax_stage(scores):
        out = []
        for sc in scores:
            m = jnp.max(sc, axis=-1, keepdims=True)
            e = jnp.exp2(sc - m)
            out.append((e.astype(BF16), m, jnp.sum(e, axis=-1, keepdims=True)))
        return tuple(out)

    def values_stage(gi, stats):
        rs, _, k0s = window_starts(gi)
        for r, k0, (e, m_l, l_l) in zip(rs, k0s, stats):
            q0 = pl.multiple_of(r * GRID_W, GRID_W)
            o_l = jnp.dot(e, v_ref[pl.ds(k0, kr * GRID_W), cols].astype(BF16), preferred_element_type=F32)
            m_c = mc_ref[pl.ds(q0, GRID_W), :]
            m = jnp.maximum(m_l, m_c)
            a_l = jnp.exp2(m_l - m)
            a_c = jnp.exp2(m_c - m)
            o = (a_l * o_l + a_c * oc_ref[pl.ds(q0, GRID_W), :]) / (a_l * l_l + a_c * lc_ref[pl.ds(q0, GRID_W), :])
            g = gate_ref[pl.ds(q0, GRID_W), cols].astype(F32)
            o_ref[pl.ds(q0, GRID_W), cols] = (o * _silu(g)).astype(o_ref.dtype)

    def pipelined(gi, stats):
        nxt = softmax_stage(scores_stage(gi + 1))
        values_stage(gi, stats)
        return nxt

    n_groups = rows // group
    last = lax.fori_loop(0, n_groups - 1, pipelined, softmax_stage(scores_stage(jnp.int32(0))))
    values_stage(jnp.int32(n_groups - 1), last)


def _branch_d_latent(p, seq_len, cache_k, cache_v, bias_tab, layer):
    t = p.shape[0]
    wb = p.shape[1] // N_IN_SLICES
    nh, n_off = bias_tab.shape[1], bias_tab.shape[2]
    hd = wb // nh
    past = cache_k.shape[2]
    rows = seq_len // GRID_W
    kr = min(n_off // 2 + 1, rows)
    assert kr % 2 == 0
    hp = ATTN_HEADS_PER_STEP if nh % ATTN_HEADS_PER_STEP == 0 else 1
    steps = nh // hp
    col = lambda s: pl.BlockSpec((seq_len, hp * hd), lambda b, h, s=s: (b, s * steps + h))
    ctx = pl.BlockSpec((None, None, past, hp * hd), lambda b, h: (b, layer, 0, h))
    return pl.pallas_call(
        functools.partial(_nbr_attn_kernel, rows=rows, kr=kr, hd=hd),
        grid=(t // seq_len, steps),
        in_specs=[col(10), col(11), col(12), col(13), ctx, ctx,
                  pl.BlockSpec((None, hp, n_off, GRID_W, 2 * GRID_W), lambda b, h: (layer, h, 0, 0, 0))],
        out_specs=pl.BlockSpec((seq_len, hp * hd), lambda b, h: (b, h)),
        out_shape=jax.ShapeDtypeStruct((t, wb), BF16),
        scratch_shapes=[pltpu.VMEM((seq_len, hd), F32), pltpu.VMEM((seq_len, 1), F32), pltpu.VMEM((seq_len, 1), F32)],
        compiler_params=_params("arbitrary", "arbitrary"),
        name="branch_d_neighbourhood_attention",
    )(p, p, p, p, cache_k, cache_v, bias_tab)


def _neighbourhood_bias(rpb):
    win_c = (rpb.shape[3] + 1) // 2
    qc = np.arange(GRID_W)[:, None]
    kc = np.arange(GRID_W)[None, :]
    sc = np.clip(qc - win_c // 2, 0, GRID_W - win_c)
    valid = (kc >= sc) & (kc < sc + win_c)
    select = (kc - qc + win_c - 1 == np.arange(2 * win_c - 1)[:, None, None]) & valid
    cols = jnp.einsum("lhrd,dqk->lhrqk", rpb.astype(F32), jnp.asarray(select, F32), precision=lax.Precision.HIGHEST)
    cols = jnp.where(valid, cols * LOG2E, NEG)
    return jnp.concatenate([cols[:, :, :-1], cols[:, :, 1:]], axis=-1)


def kernel(x_prompt, x_sample, cache_k, cache_v, c, c_ctx, w_ada, b_ada, g_pre, g_post, w_in, a_conv_w, a_conv_b,
           a_ln_g, a_ln_b, b_ln_g, b_ln_b, b_ws, b_bias, c_conv_w, d_rpb, w_out):
    batch, seq, d = x_prompt.shape
    dec_batch, dec_seq, _ = x_sample.shape
    depth = w_in.shape[0]
    wb = d // 4
    nh, hd = cache_k.shape[3], cache_k.shape[4]
    past = cache_k.shape[2]
    assert nh * hd == wb and w_in.shape[2] == N_IN_SLICES * wb and dec_seq % GRID_W == 0

    n_rows = -(-(1 + dec_batch) // 8) * 8
    cond = jnp.zeros((n_rows, d), F32).at[0].set(c_ctx).at[1:1 + dec_batch].set(c)
    mod = _ada_modulation(cond, w_ada, b_ada).reshape(depth, n_rows, 3, 1, d)

    vec = lambda a: a.reshape(depth, 1, a.shape[-1])
    g_pre, g_post, a_conv_b, a_ln_g, a_ln_b, b_ln_g, b_ln_b = map(
        vec, (g_pre, g_post, a_conv_b, a_ln_g, a_ln_b, b_ln_g, b_ln_b))
    ws_b = b_ws.astype(BF16)
    gh = wb // b_ws.shape[1]
    bias_full = jnp.repeat(jnp.swapaxes(b_bias, 1, 2), gh, axis=2)
    bias_tab = _neighbourhood_bias(d_rpb)
    ck = cache_k.reshape(dec_batch, depth, past, wb)
    cv = cache_v.reshape(dec_batch, depth, past, wb)

    xs = [x_prompt.reshape(batch * seq, d), x_sample.reshape(dec_batch * dec_seq, d)]
    seqs = [seq, dec_seq]
    row0 = [0, 1]
    mod_seq = [batch * seq, dec_seq]
    p_dtype = [F32, BF16]
    hn = [_prenorm(xs[i], g_pre, mod, 0, row0[i], mod_seq[i]) for i in range(2)]
    new_k = jnp.zeros((batch, depth, seq, wb), F32)
    new_v = jnp.zeros((batch, depth, seq, wb), F32)
    w_in_l, w_out_l = w_in[0].astype(BF16), w_out[0].astype(BF16)
    lat_rows = xs[1].shape[0]
    lat_row_tiles = lat_rows // _proj_tiles(lat_rows, d)[0]
    for l in range(depth):
        has_next = l + 1 < depth

        def branches(i, p):
            return _seq_branches(p, seqs[i], l, a_conv_w, a_conv_b, a_ln_g, a_ln_b, b_ln_g, b_ln_b, ws_b, bias_full,
                                 c_conv_w)

        p = _in_proj(hn[0], w_in_l, p_dtype[0])[0]
        yd, new_k, new_v = _branch_d_context(p, seq, nh, l, new_k, new_v)
        y = _out_proj((*branches(0, p), yd), w_out_l)[0]
        xs[0], hn[0] = _post(xs[0], y, g_post, g_pre, mod, l, row0[0], mod_seq[0], has_next)

        rider = lambda w: (_weight_round_rider(w, l + 1, lat_row_tiles, _proj_tiles(lat_rows, w.shape[2])[1]),) \
            if has_next else ()
        p, *w_in_next = _in_proj(hn[1], w_in_l, p_dtype[1], rider(w_in))
        yd = _branch_d_latent(p, dec_seq, ck, cv, bias_tab, l)
        y, *w_out_next = _out_proj((*branches(1, p), yd), w_out_l, rider(w_out))
        xs[1], hn[1] = _post(xs[1], y, g_post, g_pre, mod, l, row0[1], mod_seq[1], has_next)
        if has_next:
            w_in_l, w_out_l = w_in_next[0], w_out_next[0]
    return (xs[0].reshape(batch, seq, d), xs[1].reshape(dec_batch, dec_seq, d),
            new_k.reshape(batch, depth, seq, nh, hd), new_v.reshape(batch, depth, seq, nh, hd))
```

```python
import collections
import functools

import jax
import jax.numpy as jnp
import numpy as np
from jax import lax
from jax.experimental import pallas as pl
from jax.experimental.pallas import tpu as pltpu

GRID_W = 64
EPS = 1e-6
NEG = -1e30
LOG2E = 1.4426950408889634
N_IN_SLICES = 14
HALO = 16
VMEM_LIMIT_BYTES = 56 * 1024 * 1024
MM_TILE = 1024
SEQ_TILE = 512
NORM_ROWS = 256
CONV_ROWS = 64
ATTN_HEADS_PER_STEP = 2
ATTN_ROW_GROUP = 4
LANES = 128
SUBLANES = 8

F32 = jnp.float32
BF16 = jnp.bfloat16


def _tile(n, target, unit=128):
    if n <= target:
        return n
    t = target - target % unit
    while n % t:
        t -= unit
    return t


def _params(*sem):
    return pltpu.CompilerParams(dimension_semantics=sem, vmem_limit_bytes=VMEM_LIMIT_BYTES)


def _silu(x):
    return x * jax.nn.sigmoid(x)


def _rms(x, g):
    return x * lax.rsqrt(jnp.mean(x * x, axis=-1, keepdims=True) + EPS) * g


def _ln(x, g, b):
    mu = jnp.mean(x, axis=-1, keepdims=True)
    xc = x - mu
    var = jnp.mean(xc * xc, axis=-1, keepdims=True)
    return xc * lax.rsqrt(var + EPS) * g + b


def _ada_kernel(cond_ref, w_ref, b_ref, o_ref):
    a = _silu(cond_ref[...]).astype(BF16)
    o_ref[...] = jnp.dot(a, w_ref[...].astype(BF16), preferred_element_type=F32) + b_ref[...]


def _ada_modulation(cond, w_ada, b_ada):
    depth, d, n = w_ada.shape
    r = cond.shape[0]
    tn = _tile(n, 512)
    return pl.pallas_call(
        _ada_kernel,
        grid=(depth, n // tn),
        in_specs=[pl.BlockSpec((r, d), lambda l, j: (0, 0)),
                  pl.BlockSpec((None, d, tn), lambda l, j: (l, 0, j)),
                  pl.BlockSpec((None, 1, tn), lambda l, j: (l, 0, j))],
        out_specs=pl.BlockSpec((None, r, tn), lambda l, j: (l, 0, j)),
        out_shape=jax.ShapeDtypeStruct((depth, r, n), F32),
        compiler_params=_params("arbitrary", "arbitrary"),
        name="ada_modulation",
    )(cond, w_ada, b_ada.reshape(depth, 1, n))


def _prenorm_kernel(x_ref, g_ref, mod_ref, o_ref):
    y = _rms(x_ref[...], g_ref[...])
    o_ref[...] = (y * (1.0 + mod_ref[1]) + mod_ref[0]).astype(o_ref.dtype)


def _post_kernel(x_ref, y_ref, gpost_ref, mod_ref, *rest, has_next):
    xn = x_ref[...] + mod_ref[2] * _rms(y_ref[...].astype(F32), gpost_ref[...])
    if has_next:
        gpre_ref, modn_ref, xo_ref, hn_ref = rest
        hn_ref[...] = (_rms(xn, gpre_ref[...]) * (1.0 + modn_ref[1]) + modn_ref[0]).astype(hn_ref.dtype)
    else:
        (xo_ref,) = rest
    xo_ref[...] = xn


def _row_tile(t):
    return min(NORM_ROWS, t)


def _vec_spec(layer, width):
    return pl.BlockSpec((None, 1, width), lambda *_: (layer, 0, 0))


def _mod_spec(layer, row0, tiles_per_row, d):
    return pl.BlockSpec((None, None, 3, 1, d), lambda i: (layer, row0 + i // tiles_per_row, 0, 0, 0))


Rider = collections.namedtuple("Rider", "fn in_specs args out_specs out_shape")


def _run_riders(riders, in_refs, out_refs):
    for r in riders:
        n_in, n_out = len(r.in_specs), len(r.out_specs)
        r.fn(*in_refs[:n_in], *out_refs[:n_out])
        in_refs, out_refs = in_refs[n_in:], out_refs[n_out:]


def _prenorm(x, g, mod, layer, row0, seq_len):
    t, d = x.shape
    tm = _row_tile(seq_len)
    return pl.pallas_call(
        _prenorm_kernel,
        grid=(t // tm,),
        in_specs=[pl.BlockSpec((tm, d), lambda i: (i, 0)),
                  _vec_spec(layer, d),
                  _mod_spec(layer, row0, seq_len // tm, d)],
        out_specs=pl.BlockSpec((tm, d), lambda i: (i, 0)),
        out_shape=jax.ShapeDtypeStruct((t, d), BF16),
        compiler_params=_params("arbitrary"),
        name="prenorm",
    )(x, g, mod)


def _post(x, y, g_post, g_pre, mod, layer, row0, seq_len, has_next):
    t, d = x.shape
    tm = _row_tile(seq_len)
    tpr = seq_len // tm
    row = pl.BlockSpec((tm, d), lambda i: (i, 0))
    in_specs = [row, row, _vec_spec(layer, d), _mod_spec(layer, row0, tpr, d)]
    args = [x, y, g_post, mod]
    out_specs = [row]
    out_shape = [jax.ShapeDtypeStruct((t, d), F32)]
    if has_next:
        in_specs += [_vec_spec(layer + 1, d), _mod_spec(layer + 1, row0, tpr, d)]
        args += [g_pre, mod]
        out_specs.append(row)
        out_shape.append(jax.ShapeDtypeStruct((t, d), BF16))
    out = pl.pallas_call(
        functools.partial(_post_kernel, has_next=has_next),
        grid=(t // tm,),
        in_specs=in_specs,
        out_specs=out_specs,
        out_shape=out_shape,
        compiler_params=_params("arbitrary"),
        name="post",
    )(*args)
    return (out[0], out[1]) if has_next else (out[0], None)


def _round_kernel(w_ref, o_ref):
    o_ref[...] = w_ref[...].astype(o_ref.dtype)


def _weight_round_rider(w_all, layer, grid_rows, tn):
    k = w_all.shape[1]
    rows = k // grid_rows
    assert rows * grid_rows == k and rows % (2 * SUBLANES) == 0
    return Rider(_round_kernel,
                 [pl.BlockSpec((None, rows, tn), lambda i, j: (layer, i, j))], [w_all],
                 [pl.BlockSpec((rows, tn), lambda i, j: (i, j))], [jax.ShapeDtypeStruct(w_all.shape[1:], BF16)])


def _rider_lists(riders):
    cat = lambda field: [v for r in riders for v in getattr(r, field)]
    return cat("in_specs"), cat("args"), cat("out_specs"), cat("out_shape")


def _in_proj_kernel(x_ref, w_ref, *rest, riders):
    n_in = sum(len(r.in_specs) for r in riders)
    o_ref = rest[n_in]
    o_ref[...] = jnp.dot(x_ref[...], w_ref[...], preferred_element_type=F32).astype(o_ref.dtype)
    _run_riders(riders, rest[:n_in], rest[n_in + 1:])


def _proj_tiles(t, n):
    return _tile(t, MM_TILE), _tile(n, MM_TILE)


def _in_proj(hn, w, out_dtype, riders=()):
    t, d = hn.shape
    n = w.shape[1]
    tm, tn = _proj_tiles(t, n)
    r_in, r_args, r_out, r_shape = _rider_lists(riders)
    return pl.pallas_call(
        functools.partial(_in_proj_kernel, riders=riders),
        grid=(t // tm, n // tn),
        in_specs=[pl.BlockSpec((tm, d), lambda i, j: (i, 0)), pl.BlockSpec((d, tn), lambda i, j: (0, j)), *r_in],
        out_specs=[pl.BlockSpec((tm, tn), lambda i, j: (i, j)), *r_out],
        out_shape=[jax.ShapeDtypeStruct((t, n), out_dtype), *r_shape],
        compiler_params=_params("arbitrary", "arbitrary"),
        name="in_proj",
    )(hn, w, *r_args)


def _out_proj_kernel(a_ref, b_ref, c_ref, d_ref, w_ref, *rest, riders):
    n_in = sum(len(r.in_specs) for r in riders)
    o_ref, cat_ref = rest[n_in], rest[-1]
    wb = a_ref.shape[1]

    @pl.when(pl.program_id(1) == 0)
    def _():
        for s, ref in enumerate((a_ref, b_ref, c_ref, d_ref)):
            cat_ref[:, s * wb:(s + 1) * wb] = ref[...]

    o_ref[...] = jnp.dot(cat_ref[...], w_ref[...], preferred_element_type=F32).astype(o_ref.dtype)
    _run_riders(riders, rest[:n_in], rest[n_in + 1:-1])


def _out_proj(branches, w, riders=()):
    t, wb = branches[0].shape
    k, n = w.shape
    tm, tn = _proj_tiles(t, n)
    slab = pl.BlockSpec((tm, wb), lambda i, j: (i, 0))
    r_in, r_args, r_out, r_shape = _rider_lists(riders)
    return pl.pallas_call(
        functools.partial(_out_proj_kernel, riders=riders),
        grid=(t // tm, n // tn),
        in_specs=[slab, slab, slab, slab, pl.BlockSpec((k, tn), lambda i, j: (0, j)), *r_in],
        out_specs=[pl.BlockSpec((tm, tn), lambda i, j: (i, j)), *r_out],
        out_shape=[jax.ShapeDtypeStruct((t, n), BF16), *r_shape],
        scratch_shapes=[pltpu.VMEM((tm, k), BF16)],
        compiler_params=_params("arbitrary", "arbitrary"),
        name="out_proj",
    )(*branches, w, *r_args)


def _seq_specs(tl, wb, total_rows, cols):
    r = tl // HALO
    last = total_rows // HALO - 1
    main = pl.BlockSpec((tl, wb), lambda i: (i, cols))
    prev = pl.BlockSpec((HALO, wb), lambda i: (jnp.maximum(i * r - 1, 0), cols))
    nxt = pl.BlockSpec((HALO, wb), lambda i: (jnp.minimum((i + 1) * r, last), cols))
    return main, prev, nxt


def _fill_halo(z_ref, fn, main, prev, nxt, tl, tiles_per_seq):
    i = pl.program_id(0) % tiles_per_seq
    z_ref[0:HALO, :] = jnp.where(i != 0, fn(*[r[...].astype(F32) for r in prev]), 0.0)
    z_ref[HALO:HALO + tl, :] = fn(*[r[...].astype(F32) for r in main])
    z_ref[HALO + tl:, :] = jnp.where(i != tiles_per_seq - 1, fn(*[r[...].astype(F32) for r in nxt]), 0.0)


def _conv_block(z_ref, cw_ref, r0, rc, lanes):
    taps = cw_ref.shape[0]
    first = HALO - (taps - 1) // 2
    win = z_ref[pl.ds(r0, rc + 2 * HALO), lanes]
    acc = None
    for s in range(SUBLANES):
        offs = [o for o in range(first, first + taps) if o % SUBLANES == s]
        if offs:
            rolled = pltpu.roll(win, win.shape[0] - s, 0) if s else win
            for o in offs:
                term = cw_ref[o - first:o - first + 1, lanes] * rolled[o - s:o - s + rc]
                acc = term if acc is None else acc + term
    return acc


def _conformer_kernel(val, val_p, val_n, glu, glu_p, glu_n, gate_ref, cw_ref, cb_ref, g_ref, b_ref, o_ref, z_ref,
                      y_ref, *, tl, tiles_per_seq):
    _fill_halo(z_ref, lambda v, g: v * jax.nn.sigmoid(g), (val, glu), (val_p, glu_p), (val_n, glu_n), tl, tiles_per_seq)
    wb = o_ref.shape[1]
    rc = min(CONV_ROWS, tl)

    def chunk(ci, carry):
        r0 = pl.multiple_of(ci * rc, rc)
        for cb in range(wb // LANES):
            lanes = slice(cb * LANES, (cb + 1) * LANES)
            y_ref[pl.ds(r0, rc), lanes] = _conv_block(z_ref, cw_ref, r0, rc, lanes) + cb_ref[:, lanes]
        y = _silu(_ln(y_ref[pl.ds(r0, rc), :], g_ref[...], b_ref[...]))
        o_ref[pl.ds(r0, rc), :] = (y * _silu(gate_ref[pl.ds(r0, rc), :].astype(F32))).astype(o_ref.dtype)
        return carry

    lax.fori_loop(0, tl // rc, chunk, 0)


def _gmlp_kernel(u_ref, v_ref, gate_ref, g_ref, b_ref, ws_ref, bias_ref, o_ref, *, tl):
    nh, chunk = ws_ref.shape[0], ws_ref.shape[1]
    hd = o_ref.shape[1] // nh
    vn = _ln(v_ref[...].astype(F32), g_ref[...], b_ref[...]).astype(BF16)
    for c in range(tl // chunk):
        rows = slice(c * chunk, (c + 1) * chunk)
        for h in range(nh):
            cols = slice(h * hd, (h + 1) * hd)
            s = jnp.dot(ws_ref[h], vn[rows, cols], preferred_element_type=F32) + bias_ref[:, cols]
            y = u_ref[rows, cols].astype(F32) * s * _silu(gate_ref[rows, cols].astype(F32))
            o_ref[rows, cols] = y.astype(o_ref.dtype)


def _short_conv_kernel(cc, cc_p, cc_n, cx, cx_p, cx_n, cb_ref, gate_ref, cw_ref, o_ref, z_ref, *, tl, tiles_per_seq):
    _fill_halo(z_ref, lambda a, b: a * b, (cc, cx), (cc_p, cx_p), (cc_n, cx_n), tl, tiles_per_seq)
    wb = o_ref.shape[1]
    rc = min(CONV_ROWS, tl)

    def chunk(ci, carry):
        r0 = pl.multiple_of(ci * rc, rc)
        for blk in range(wb // LANES):
            lanes = slice(blk * LANES, (blk + 1) * LANES)
            y = cb_ref[pl.ds(r0, rc), lanes].astype(F32) * _conv_block(z_ref, cw_ref, r0, rc, lanes)
            o_ref[pl.ds(r0, rc), lanes] = (y * _silu(gate_ref[pl.ds(r0, rc), lanes].astype(F32))).astype(o_ref.dtype)
        return carry

    lax.fori_loop(0, tl // rc, chunk, 0)


_N_CONFORMER_IN, _N_GMLP_IN, _N_SHORT_CONV_IN, _N_CTX_ATTN_IN = 11, 7, 9, 6


def _seq_branches_kernel(*refs, tl, tiles_per_seq, attn_heads):
    a_in, refs = refs[:_N_CONFORMER_IN], refs[_N_CONFORMER_IN:]
    b_in, refs = refs[:_N_GMLP_IN], refs[_N_GMLP_IN:]
    c_in, refs = refs[:_N_SHORT_CONV_IN], refs[_N_SHORT_CONV_IN:]
    d_in, refs = (refs[:_N_CTX_ATTN_IN], refs[_N_CTX_ATTN_IN:]) if attn_heads else ((), refs)
    (oa_ref, ob_ref, oc_ref), d_out, (z_ref, y_ref) = refs[:3], refs[3:-2], refs[-2:]
    _conformer_kernel(*a_in, oa_ref, z_ref, y_ref, tl=tl, tiles_per_seq=tiles_per_seq)
    _gmlp_kernel(*b_in, ob_ref, tl=tl)
    _short_conv_kernel(*c_in, oc_ref, z_ref, tl=tl, tiles_per_seq=tiles_per_seq)
    if attn_heads:
        _ctx_attn_kernel(*d_in, *d_out, nh=attn_heads)


def _seq_branches(p, seq_len, layer, a_conv_w, a_conv_b, a_ln_g, a_ln_b, b_ln_g, b_ln_b, ws, bias_full, c_conv_w,
                  ctx_attn=None):
    t = p.shape[0]
    wb = p.shape[1] // N_IN_SLICES
    tl = _tile(seq_len, SEQ_TILE)
    nh, chunk = ws.shape[1], ws.shape[2]
    vec = _vec_spec(layer, wb)
    tile = lambda s: pl.BlockSpec((tl, wb), lambda i: (i, s))
    taps = lambda w: pl.BlockSpec((None, w.shape[1], wb), lambda i: (layer, 0, 0))
    a_specs = [*_seq_specs(tl, wb, t, 0), *_seq_specs(tl, wb, t, 1), tile(2), taps(a_conv_w), vec, vec, vec]
    b_specs = [tile(3), tile(4), tile(5), vec, vec,
               pl.BlockSpec((None, nh, chunk, chunk), lambda i: (layer, 0, 0, 0)),
               pl.BlockSpec((None, chunk, wb), lambda i: (layer, 0, 0))]
    c_specs = [*_seq_specs(tl, wb, t, 7), *_seq_specs(tl, wb, t, 8), tile(6), tile(9), taps(c_conv_w)]
    assert (len(a_specs), len(b_specs), len(c_specs)) == (_N_CONFORMER_IN, _N_GMLP_IN, _N_SHORT_CONV_IN)
    slab = pl.BlockSpec((tl, wb), lambda i: (i, 0))
    in_specs = [*a_specs, *b_specs, *c_specs]
    args = [*[p] * 7, a_conv_w, a_conv_b, a_ln_g, a_ln_b, *[p] * 3, b_ln_g, b_ln_b, ws, bias_full, *[p] * 8, c_conv_w]
    out_specs = [slab, slab, slab]
    out_shape = [jax.ShapeDtypeStruct((t, wb), BF16)] * 3
    aliases = {}
    if ctx_attn is not None:
        heads, new_k, new_v = ctx_attn
        assert tl == seq_len
        kv_out = pl.BlockSpec((None, None, seq_len, wb), lambda b: (b, layer, 0, 0))
        any_space = pl.BlockSpec(memory_space=pl.ANY)
        aliases = {len(in_specs) + 4: 4, len(in_specs) + 5: 5}
        in_specs += [tile(10), tile(11), tile(12), tile(13), any_space, any_space]
        args += [p, p, p, p, new_k, new_v]
        out_specs += [slab, kv_out, kv_out]
        out_shape += [jax.ShapeDtypeStruct((t, wb), BF16), jax.ShapeDtypeStruct(new_k.shape, new_k.dtype),
                      jax.ShapeDtypeStruct(new_v.shape, new_v.dtype)]
    return pl.pallas_call(
        functools.partial(_seq_branches_kernel, tl=tl, tiles_per_seq=seq_len // tl,
                          attn_heads=ctx_attn[0] if ctx_attn else 0),
        grid=(t // tl,),
        in_specs=in_specs,
        out_specs=out_specs,
        out_shape=out_shape,
        input_output_aliases=aliases,
        scratch_shapes=[pltpu.VMEM((tl + 2 * HALO, wb), F32), pltpu.VMEM((tl, wb), F32)],
        compiler_params=_params("arbitrary"),
        name="branches_abcd" if ctx_attn else "branches_abc",
    )(*args)


_NT = (((1,), (1,)), ((), ()))


def _ctx_attn_kernel(q_ref, k_ref, v_ref, gate_ref, kall_ref, vall_ref, o_ref, knew_ref, vnew_ref, *, nh):
    del kall_ref, vall_ref
    knew_ref[...] = k_ref[...].astype(knew_ref.dtype)
    vnew_ref[...] = v_ref[...].astype(vnew_ref.dtype)
    hd = o_ref.shape[1] // nh
    scale = hd ** -0.5
    for h in range(nh):
        cols = slice(h * hd, (h + 1) * hd)
        q = q_ref[:, cols].astype(BF16)
        s = lax.dot_general(q, k_ref[:, cols].astype(BF16), _NT, preferred_element_type=F32) * scale
        e = jnp.exp(s - jnp.max(s, axis=-1, keepdims=True))
        o = jnp.dot(e.astype(BF16), v_ref[:, cols].astype(BF16), preferred_element_type=F32)
        o = o / jnp.sum(e, axis=-1, keepdims=True)
        o_ref[:, cols] = (o * _silu(gate_ref[:, cols].astype(F32))).astype(o_ref.dtype)


def _nbr_attn_kernel(q_ref, k_ref, v_ref, gate_ref, ck_ref, cv_ref, bias_ref, o_ref, oc_ref, mc_ref, lc_ref,
                     *, rows, kr, hd):
    for h in range(o_ref.shape[1] // hd):
        cols = slice(h * hd, (h + 1) * hd)
        _nbr_attn_head(q_ref, k_ref, v_ref, gate_ref, ck_ref, cv_ref, bias_ref.at[h], o_ref, oc_ref, mc_ref, lc_ref,
                       cols, rows, kr)


def _nbr_attn_head(q_ref, k_ref, v_ref, gate_ref, ck_ref, cv_ref, bias_ref, o_ref, oc_ref, mc_ref, lc_ref,
                   cols, rows, kr):
    hd = cols.stop - cols.start
    seq_len = q_ref.shape[0]
    win_r = bias_ref.shape[0] // 2 + 1
    scale = hd ** -0.5 * LOG2E
    ck = ck_ref[:, cols].astype(BF16)
    cv = cv_ref[:, cols].astype(BF16)

    cq = _tile(seq_len, 512)

    def ctx_chunk(ci, carry):
        r0 = pl.multiple_of(ci * cq, cq)
        s = lax.dot_general(q_ref[pl.ds(r0, cq), cols].astype(BF16), ck, _NT, preferred_element_type=F32) * scale
        m = jnp.max(s, axis=-1, keepdims=True)
        e = jnp.exp2(s - m)
        mc_ref[pl.ds(r0, cq), :] = m
        lc_ref[pl.ds(r0, cq), :] = jnp.sum(e, axis=-1, keepdims=True)
        oc_ref[pl.ds(r0, cq), :] = jnp.dot(e.astype(BF16), cv, preferred_element_type=F32)
        return carry

    lax.fori_loop(0, seq_len // cq, ctx_chunk, 0, unroll=True)

    group = ATTN_ROW_GROUP if rows % ATTN_ROW_GROUP == 0 else 1

    def window_starts(gi):
        rs = [gi * group + i for i in range(group)]
        starts = [jnp.clip(r - kr // 2, 0, rows - kr) for r in rs]
        return rs, starts, [pl.multiple_of(s * GRID_W, GRID_W) for s in starts]

    def scores_stage(gi):
        rs, starts, k0s = window_starts(gi)
        out = []
        for r, s, k0 in zip(rs, starts, k0s):
            q0 = pl.multiple_of(r * GRID_W, GRID_W)
            base = win_r - 1 - (r - s)
            bias = jnp.concatenate([bias_ref[base + 2 * t] for t in range(kr // 2)], axis=1)
            out.append(lax.dot_general(q_ref[pl.ds(q0, GRID_W), cols].astype(BF16),
                                       k_ref[pl.ds(k0, kr * GRID_W), cols].astype(BF16), _NT,
                                       preferred_element_type=F32) * scale + bias)
        return tuple(out)

    def softmax_stage(scores):
        out = []
        for sc in scores:
            m = jnp.max(sc, axis=-1, keepdims=True)
            e = jnp.exp2(sc - m)
            out.append((e.astype(BF16), m, jnp.sum(e, axis=-1, keepdims=True)))
        return tuple(out)

    def values_stage(gi, stats):
        rs, _, k0s = window_starts(gi)
        for r, k0, (e, m_l, l_l) in zip(rs, k0s, stats):
            q0 = pl.multiple_of(r * GRID_W, GRID_W)
            o_l = jnp.dot(e, v_ref[pl.ds(k0, kr * GRID_W), cols].astype(BF16), preferred_element_type=F32)
            m_c = mc_ref[pl.ds(q0, GRID_W), :]
            m = jnp.maximum(m_l, m_c)
            a_l = jnp.exp2(m_l - m)
            a_c = jnp.exp2(m_c - m)
            o = (a_l * o_l + a_c * oc_ref[pl.ds(q0, GRID_W), :]) / (a_l * l_l + a_c * lc_ref[pl.ds(q0, GRID_W), :])
            g = gate_ref[pl.ds(q0, GRID_W), cols].astype(F32)
            o_ref[pl.ds(q0, GRID_W), cols] = (o * _silu(g)).astype(o_ref.dtype)

    def pipelined(gi, stats):
        nxt = softmax_stage(scores_stage(gi + 1))
        values_stage(gi, stats)
        return nxt

    n_groups = rows // group
    last = lax.fori_loop(0, n_groups - 1, pipelined, softmax_stage(scores_stage(jnp.int32(0))))
    values_stage(jnp.int32(n_groups - 1), last)


def _branch_d_latent(p, seq_len, cache_k, cache_v, bias_tab, layer):
    t = p.shape[0]
    wb = p.shape[1] // N_IN_SLICES
    nh, n_off = bias_tab.shape[1], bias_tab.shape[2]
    hd = wb // nh
    past = cache_k.shape[2]
    rows = seq_len // GRID_W
    kr = min(n_off // 2 + 1, rows)
    assert kr % 2 == 0
    hp = ATTN_HEADS_PER_STEP if nh % ATTN_HEADS_PER_STEP == 0 else 1
    steps = nh // hp
    col = lambda s: pl.BlockSpec((seq_len, hp * hd), lambda b, h, s=s: (b, s * steps + h))
    ctx = pl.BlockSpec((None, None, past, hp * hd), lambda b, h: (b, layer, 0, h))
    return pl.pallas_call(
        functools.partial(_nbr_attn_kernel, rows=rows, kr=kr, hd=hd),
        grid=(t // seq_len, steps),
        in_specs=[col(10), col(11), col(12), col(13), ctx, ctx,
                  pl.BlockSpec((None, hp, n_off, GRID_W, 2 * GRID_W), lambda b, h: (layer, h, 0, 0, 0))],
        out_specs=pl.BlockSpec((seq_len, hp * hd), lambda b, h: (b, h)),
        out_shape=jax.ShapeDtypeStruct((t, wb), BF16),
        scratch_shapes=[pltpu.VMEM((seq_len, hd), F32), pltpu.VMEM((seq_len, 1), F32), pltpu.VMEM((seq_len, 1), F32)],
        compiler_params=_params("arbitrary", "arbitrary"),
        name="branch_d_neighbourhood_attention",
    )(p, p, p, p, cache_k, cache_v, bias_tab)


def _neighbourhood_bias(rpb):
    win_c = (rpb.shape[3] + 1) // 2
    qc = np.arange(GRID_W)[:, None]
    kc = np.arange(GRID_W)[None, :]
    sc = np.clip(qc - win_c // 2, 0, GRID_W - win_c)
    valid = (kc >= sc) & (kc < sc + win_c)
    select = (kc - qc + win_c - 1 == np.arange(2 * win_c - 1)[:, None, None]) & valid
    cols = jnp.einsum("lhrd,dqk->lhrqk", rpb.astype(F32), jnp.asarray(select, F32), precision=lax.Precision.HIGHEST)
    cols = jnp.where(valid, cols * LOG2E, NEG)
    return jnp.concatenate([cols[:, :, :-1], cols[:, :, 1:]], axis=-1)


def kernel(x_prompt, x_sample, cache_k, cache_v, c, c_ctx, w_ada, b_ada, g_pre, g_post, w_in, a_conv_w, a_conv_b,
           a_ln_g, a_ln_b, b_ln_g, b_ln_b, b_ws, b_bias, c_conv_w, d_rpb, w_out):
    batch, seq, d = x_prompt.shape
    dec_batch, dec_seq, _ = x_sample.shape
    depth = w_in.shape[0]
    wb = d // 4
    nh, hd = cache_k.shape[3], cache_k.shape[4]
    past = cache_k.shape[2]
    assert nh * hd == wb and w_in.shape[2] == N_IN_SLICES * wb and dec_seq % GRID_W == 0

    n_rows = -(-(1 + dec_batch) // 8) * 8
    cond = jnp.zeros((n_rows, d), F32).at[0].set(c_ctx).at[1:1 + dec_batch].set(c)
    mod = _ada_modulation(cond, w_ada, b_ada).reshape(depth, n_rows, 3, 1, d)

    vec = lambda a: a.reshape(depth, 1, a.shape[-1])
    g_pre, g_post, a_conv_b, a_ln_g, a_ln_b, b_ln_g, b_ln_b = map(
        vec, (g_pre, g_post, a_conv_b, a_ln_g, a_ln_b, b_ln_g, b_ln_b))
    ws_b = b_ws.astype(BF16)
    gh = wb // b_ws.shape[1]
    bias_full = jnp.repeat(jnp.swapaxes(b_bias, 1, 2), gh, axis=2)
    bias_tab = _neighbourhood_bias(d_rpb)
    ck = cache_k.reshape(dec_batch, depth, past, wb)
    cv = cache_v.reshape(dec_batch, depth, past, wb)

    xs = [x_prompt.reshape(batch * seq, d), x_sample.reshape(dec_batch * dec_seq, d)]
    seqs = [seq, dec_seq]
    row0 = [0, 1]
    mod_seq = [batch * seq, dec_seq]
    p_dtype = [F32, BF16]
    hn = [_prenorm(xs[i], g_pre, mod, 0, row0[i], mod_seq[i]) for i in range(2)]
    new_k = jnp.zeros((batch, depth, seq, wb), F32)
    new_v = jnp.zeros((batch, depth, seq, wb), F32)
    w_in_l, w_out_l = w_in[0].astype(BF16), w_out[0].astype(BF16)
    lat_rows = xs[1].shape[0]
    lat_row_tiles = lat_rows // _proj_tiles(lat_rows, d)[0]
    for l in range(depth):
        has_next = l + 1 < depth

        def branches(i, p, ctx_attn=None):
            return _seq_branches(p, seqs[i], l, a_conv_w, a_conv_b, a_ln_g, a_ln_b, b_ln_g, b_ln_b, ws_b, bias_full,
                                 c_conv_w, ctx_attn)

        p = _in_proj(hn[0], w_in_l, p_dtype[0])[0]
        *slabs, new_k, new_v = branches(0, p, (nh, new_k, new_v))
        y = _out_proj(slabs, w_out_l)[0]
        xs[0], hn[0] = _post(xs[0], y, g_post, g_pre, mod, l, row0[0], mod_seq[0], has_next)

        rider = lambda w: (_weight_round_rider(w, l + 1, lat_row_tiles, _proj_tiles(lat_rows, w.shape[2])[1]),) \
            if has_next else ()
        p, *w_in_next = _in_proj(hn[1], w_in_l, p_dtype[1], rider(w_in))
        yd = _branch_d_latent(p, dec_seq, ck, cv, bias_tab, l)
        y, *w_out_next = _out_proj((*branches(1, p), yd), w_out_l, rider(w_out))
        xs[1], hn[1] = _post(xs[1], y, g_post, g_pre, mod, l, row0[1], mod_seq[1], has_next)
        if has_next:
            w_in_l, w_out_l = w_in_next[0], w_out_next[0]
    return (xs[0].reshape(batch, seq, d), xs[1].reshape(dec_batch, dec_seq, d),
            new_k.reshape(batch, depth, seq, nh, hd), new_v.reshape(batch, depth, seq, nh, hd))
```

```python
import collections
import functools

import jax
import jax.numpy as jnp
import numpy as np
from jax import lax
from jax.experimental import pallas as pl
from jax.experimental.pallas import tpu as pltpu

GRID_W = 64
EPS = 1e-6
NEG = -1e30
LOG2E = 1.4426950408889634
N_IN_SLICES = 14
HALO = 16
VMEM_LIMIT_BYTES = 56 * 1024 * 1024
MM_TILE = 1024
SEQ_TILE = 512
NORM_ROWS = 256
CONV_ROWS = 64
ATTN_HEADS_PER_STEP = 2
ATTN_ROW_GROUP = 4
LANES = 128
SUBLANES = 8

F32 = jnp.float32
BF16 = jnp.bfloat16


def _tile(n, target, unit=128):
    if n <= target:
        return n
    t = target - target % unit
    while n % t:
        t -= unit
    return t


def _params(*sem):
    return pltpu.CompilerParams(dimension_semantics=sem, vmem_limit_bytes=VMEM_LIMIT_BYTES)


def _silu(x):
    return x * jax.nn.sigmoid(x)


def _rms(x, g):
    return x * lax.rsqrt(jnp.mean(x * x, axis=-1, keepdims=True) + EPS) * g


def _ln(x, g, b):
    mu = jnp.mean(x, axis=-1, keepdims=True)
    xc = x - mu
    var = jnp.mean(xc * xc, axis=-1, keepdims=True)
    return xc * lax.rsqrt(var + EPS) * g + b


def _ada_kernel(cond_ref, w_ref, b_ref, o_ref):
    a = _silu(cond_ref[...]).astype(BF16)
    o_ref[...] = jnp.dot(a, w_ref[...].astype(BF16), preferred_element_type=F32) + b_ref[...]


def _ada_modulation(cond, w_ada, b_ada):
    depth, d, n = w_ada.shape
    r = cond.shape[0]
    tn = _tile(n, 512)
    return pl.pallas_call(
        _ada_kernel,
        grid=(depth, n // tn),
        in_specs=[pl.BlockSpec((r, d), lambda l, j: (0, 0)),
                  pl.BlockSpec((None, d, tn), lambda l, j: (l, 0, j)),
                  pl.BlockSpec((None, 1, tn), lambda l, j: (l, 0, j))],
        out_specs=pl.BlockSpec((None, r, tn), lambda l, j: (l, 0, j)),
        out_shape=jax.ShapeDtypeStruct((depth, r, n), F32),
        compiler_params=_params("arbitrary", "arbitrary"),
        name="ada_modulation",
    )(cond, w_ada, b_ada.reshape(depth, 1, n))


def _prenorm_kernel(x_ref, g_ref, mod_ref, o_ref):
    y = _rms(x_ref[...], g_ref[...])
    o_ref[...] = (y * (1.0 + mod_ref[1]) + mod_ref[0]).astype(o_ref.dtype)


def _post_kernel(x_ref, y_ref, gpost_ref, mod_ref, *rest, has_next):
    xn = x_ref[...] + mod_ref[2] * _rms(y_ref[...].astype(F32), gpost_ref[...])
    if has_next:
        gpre_ref, modn_ref, xo_ref, hn_ref = rest
        hn_ref[...] = (_rms(xn, gpre_ref[...]) * (1.0 + modn_ref[1]) + modn_ref[0]).astype(hn_ref.dtype)
    else:
        (xo_ref,) = rest
    xo_ref[...] = xn


def _row_tile(t):
    return min(NORM_ROWS, t)


def _vec_spec(layer, width):
    return pl.BlockSpec((None, 1, width), lambda *_: (layer, 0, 0))


def _mod_spec(layer, row0, tiles_per_row, d):
    return pl.BlockSpec((None, None, 3, 1, d), lambda i: (layer, row0 + i // tiles_per_row, 0, 0, 0))


Rider = collections.namedtuple("Rider", "fn in_specs args out_specs out_shape")


def _run_riders(riders, in_refs, out_refs):
    for r in riders:
        n_in, n_out = len(r.in_specs), len(r.out_specs)
        r.fn(*in_refs[:n_in], *out_refs[:n_out])
        in_refs, out_refs = in_refs[n_in:], out_refs[n_out:]


def _prenorm(x, g, mod, layer, row0, seq_len):
    t, d = x.shape
    tm = _row_tile(seq_len)
    return pl.pallas_call(
        _prenorm_kernel,
        grid=(t // tm,),
        in_specs=[pl.BlockSpec((tm, d), lambda i: (i, 0)),
                  _vec_spec(layer, d),
                  _mod_spec(layer, row0, seq_len // tm, d)],
        out_specs=pl.BlockSpec((tm, d), lambda i: (i, 0)),
        out_shape=jax.ShapeDtypeStruct((t, d), BF16),
        compiler_params=_params("arbitrary"),
        name="prenorm",
    )(x, g, mod)


def _post(x, y, g_post, g_pre, mod, layer, row0, seq_len, has_next):
    t, d = x.shape
    tm = _row_tile(seq_len)
    tpr = seq_len // tm
    row = pl.BlockSpec((tm, d), lambda i: (i, 0))
    in_specs = [row, row, _vec_spec(layer, d), _mod_spec(layer, row0, tpr, d)]
    args = [x, y, g_post, mod]
    out_specs = [row]
    out_shape = [jax.ShapeDtypeStruct((t, d), F32)]
    if has_next:
        in_specs += [_vec_spec(layer + 1, d), _mod_spec(layer + 1, row0, tpr, d)]
        args += [g_pre, mod]
        out_specs.append(row)
        out_shape.append(jax.ShapeDtypeStruct((t, d), BF16))
    out = pl.pallas_call(
        functools.partial(_post_kernel, has_next=has_next),
        grid=(t // tm,),
        in_specs=in_specs,
        out_specs=out_specs,
        out_shape=out_shape,
        compiler_params=_params("arbitrary"),
        name="post",
    )(*args)
    return (out[0], out[1]) if has_next else (out[0], None)


def _round_kernel(w_ref, o_ref):
    o_ref[...] = w_ref[...].astype(o_ref.dtype)


def _weight_round_rider(w_all, layer, grid_rows, tn):
    k = w_all.shape[1]
    rows = k // grid_rows
    assert rows * grid_rows == k and rows % (2 * SUBLANES) == 0
    return Rider(_round_kernel,
                 [pl.BlockSpec((None, rows, tn), lambda i, j: (layer, i, j))], [w_all],
                 [pl.BlockSpec((rows, tn), lambda i, j: (i, j))], [jax.ShapeDtypeStruct(w_all.shape[1:], BF16)])


def _rider_lists(riders):
    cat = lambda field: [v for r in riders for v in getattr(r, field)]
    return cat("in_specs"), cat("args"), cat("out_specs"), cat("out_shape")


def _in_proj_kernel(x_ref, w_ref, *rest, riders):
    n_in = sum(len(r.in_specs) for r in riders)
    o_ref = rest[n_in]
    o_ref[...] = jnp.dot(x_ref[...], w_ref[...], preferred_element_type=F32).astype(o_ref.dtype)
    _run_riders(riders, rest[:n_in], rest[n_in + 1:])


def _proj_tiles(t, n):
    return _tile(t, MM_TILE), _tile(n, MM_TILE)


def _in_proj(hn, w, out_dtype, riders=()):
    t, d = hn.shape
    n = w.shape[1]
    tm, tn = _proj_tiles(t, n)
    r_in, r_args, r_out, r_shape = _rider_lists(riders)
    return pl.pallas_call(
        functools.partial(_in_proj_kernel, riders=riders),
        grid=(t // tm, n // tn),
        in_specs=[pl.BlockSpec((tm, d), lambda i, j: (i, 0)), pl.BlockSpec((d, tn), lambda i, j: (0, j)), *r_in],
        out_specs=[pl.BlockSpec((tm, tn), lambda i, j: (i, j)), *r_out],
        out_shape=[jax.ShapeDtypeStruct((t, n), out_dtype), *r_shape],
        compiler_params=_params("arbitrary", "arbitrary"),
        name="in_proj",
    )(hn, w, *r_args)


def _out_proj_kernel(a_ref, b_ref, c_ref, d_ref, w_ref, *rest, riders):
    n_in = sum(len(r.in_specs) for r in riders)
    o_ref, cat_ref = rest[n_in], rest[-1]
    wb = a_ref.shape[1]

    @pl.when(pl.program_id(1) == 0)
    def _():
        for s, ref in enumerate((a_ref, b_ref, c_ref, d_ref)):
            cat_ref[:, s * wb:(s + 1) * wb] = ref[...]

    o_ref[...] = jnp.dot(cat_ref[...], w_ref[...], preferred_element_type=F32).astype(o_ref.dtype)
    _run_riders(riders, rest[:n_in], rest[n_in + 1:-1])


def _out_proj(branches, w, riders=()):
    t, wb = branches[0].shape
    k, n = w.shape
    tm, tn = _proj_tiles(t, n)
    slab = pl.BlockSpec((tm, wb), lambda i, j: (i, 0))
    r_in, r_args, r_out, r_shape = _rider_lists(riders)
    return pl.pallas_call(
        functools.partial(_out_proj_kernel, riders=riders),
        grid=(t // tm, n // tn),
        in_specs=[slab, slab, slab, slab, pl.BlockSpec((k, tn), lambda i, j: (0, j)), *r_in],
        out_specs=[pl.BlockSpec((tm, tn), lambda i, j: (i, j)), *r_out],
        out_shape=[jax.ShapeDtypeStruct((t, n), BF16), *r_shape],
        scratch_shapes=[pltpu.VMEM((tm, k), BF16)],
        compiler_params=_params("arbitrary", "arbitrary"),
        name="out_proj",
    )(*branches, w, *r_args)


def _seq_specs(tl, wb, total_rows, cols):
    r = tl // HALO
    last = total_rows // HALO - 1
    main = pl.BlockSpec((tl, wb), lambda i: (i, cols))
    prev = pl.BlockSpec((HALO, wb), lambda i: (jnp.maximum(i * r - 1, 0), cols))
    nxt = pl.BlockSpec((HALO, wb), lambda i: (jnp.minimum((i + 1) * r, last), cols))
    return main, prev, nxt


def _fill_halo(z_ref, fn, main, prev, nxt, tl, tiles_per_seq):
    i = pl.program_id(0) % tiles_per_seq
    z_ref[0:HALO, :] = jnp.where(i != 0, fn(*[r[...].astype(F32) for r in prev]), 0.0)
    z_ref[HALO:HALO + tl, :] = fn(*[r[...].astype(F32) for r in main])
    z_ref[HALO + tl:, :] = jnp.where(i != tiles_per_seq - 1, fn(*[r[...].astype(F32) for r in nxt]), 0.0)


def _conv_block(z_ref, cw_ref, r0, rc, lanes):
    taps = cw_ref.shape[0]
    first = HALO - (taps - 1) // 2
    win = z_ref[pl.ds(r0, rc + 2 * HALO), lanes]
    acc = None
    for s in range(SUBLANES):
        offs = [o for o in range(first, first + taps) if o % SUBLANES == s]
        if offs:
            rolled = pltpu.roll(win, win.shape[0] - s, 0) if s else win
            for o in offs:
                term = cw_ref[o - first:o - first + 1, lanes] * rolled[o - s:o - s + rc]
                acc = term if acc is None else acc + term
    return acc


def _conformer_kernel(val, val_p, val_n, glu, glu_p, glu_n, gate_ref, cw_ref, cb_ref, g_ref, b_ref, o_ref, z_ref,
                      y_ref, *, tl, tiles_per_seq):
    _fill_halo(z_ref, lambda v, g: v * jax.nn.sigmoid(g), (val, glu), (val_p, glu_p), (val_n, glu_n), tl, tiles_per_seq)
    wb = o_ref.shape[1]
    rc = min(CONV_ROWS, tl)

    def chunk(ci, carry):
        r0 = pl.multiple_of(ci * rc, rc)
        for cb in range(wb // LANES):
            lanes = slice(cb * LANES, (cb + 1) * LANES)
            y_ref[pl.ds(r0, rc), lanes] = _conv_block(z_ref, cw_ref, r0, rc, lanes) + cb_ref[:, lanes]
        y = _silu(_ln(y_ref[pl.ds(r0, rc), :], g_ref[...], b_ref[...]))
        o_ref[pl.ds(r0, rc), :] = (y * _silu(gate_ref[pl.ds(r0, rc), :].astype(F32))).astype(o_ref.dtype)
        return carry

    lax.fori_loop(0, tl // rc, chunk, 0)


def _gmlp_kernel(u_ref, v_ref, gate_ref, g_ref, b_ref, ws_ref, bias_ref, o_ref, *, tl):
    nh, chunk = ws_ref.shape[0], ws_ref.shape[1]
    hd = o_ref.shape[1] // nh
    vn = _ln(v_ref[...].astype(F32), g_ref[...], b_ref[...]).astype(BF16)
    for c in range(tl // chunk):
        rows = slice(c * chunk, (c + 1) * chunk)
        for h in range(nh):
            cols = slice(h * hd, (h + 1) * hd)
            s = jnp.dot(ws_ref[h], vn[rows, cols], preferred_element_type=F32) + bias_ref[:, cols]
            y = u_ref[rows, cols].astype(F32) * s * _silu(gate_ref[rows, cols].astype(F32))
            o_ref[rows, cols] = y.astype(o_ref.dtype)


def _short_conv_kernel(cc, cc_p, cc_n, cx, cx_p, cx_n, cb_ref, gate_ref, cw_ref, o_ref, z_ref, *, tl, tiles_per_seq):
    _fill_halo(z_ref, lambda a, b: a * b, (cc, cx), (cc_p, cx_p), (cc_n, cx_n), tl, tiles_per_seq)
    wb = o_ref.shape[1]
    rc = min(CONV_ROWS, tl)

    def chunk(ci, carry):
        r0 = pl.multiple_of(ci * rc, rc)
        for blk in range(wb // LANES):
            lanes = slice(blk * LANES, (blk + 1) * LANES)
            y = cb_ref[pl.ds(r0, rc), lanes].astype(F32) * _conv_block(z_ref, cw_ref, r0, rc, lanes)
            o_ref[pl.ds(r0, rc), lanes] = (y * _silu(gate_ref[pl.ds(r0, rc), lanes].astype(F32))).astype(o_ref.dtype)
        return carry

    lax.fori_loop(0, tl // rc, chunk, 0)


_N_CONFORMER_IN, _N_GMLP_IN, _N_SHORT_CONV_IN, _N_CTX_ATTN_IN = 11, 7, 9, 6


def _seq_branches_kernel(*refs, tl, tiles_per_seq, attn_heads, n_attn_in, layer):
    a_in, refs = refs[:_N_CONFORMER_IN], refs[_N_CONFORMER_IN:]
    b_in, refs = refs[:_N_GMLP_IN], refs[_N_GMLP_IN:]
    c_in, refs = refs[:_N_SHORT_CONV_IN], refs[_N_SHORT_CONV_IN:]
    d_in, refs = refs[:n_attn_in], refs[n_attn_in:]
    (oa_ref, ob_ref, oc_ref), d_out, (z_ref, y_ref) = refs[:3], refs[3:-2], refs[-2:]
    _conformer_kernel(*a_in, oa_ref, z_ref, y_ref, tl=tl, tiles_per_seq=tiles_per_seq)
    _gmlp_kernel(*b_in, ob_ref, tl=tl)
    _short_conv_kernel(*c_in, oc_ref, z_ref, tl=tl, tiles_per_seq=tiles_per_seq)
    if attn_heads:
        _ctx_attn_kernel(*d_in, *d_out, nh=attn_heads, layer=layer)


def _seq_branches(p, seq_len, layer, a_conv_w, a_conv_b, a_ln_g, a_ln_b, b_ln_g, b_ln_b, ws, bias_full, c_conv_w,
                  ctx_attn=None):
    t = p.shape[0]
    wb = p.shape[1] // N_IN_SLICES
    tl = _tile(seq_len, SEQ_TILE)
    nh, chunk = ws.shape[1], ws.shape[2]
    vec = _vec_spec(layer, wb)
    tile = lambda s: pl.BlockSpec((tl, wb), lambda i: (i, s))
    taps = lambda w: pl.BlockSpec((None, w.shape[1], wb), lambda i: (layer, 0, 0))
    a_specs = [*_seq_specs(tl, wb, t, 0), *_seq_specs(tl, wb, t, 1), tile(2), taps(a_conv_w), vec, vec, vec]
    b_specs = [tile(3), tile(4), tile(5), vec, vec,
               pl.BlockSpec((None, nh, chunk, chunk), lambda i: (layer, 0, 0, 0)),
               pl.BlockSpec((None, chunk, wb), lambda i: (layer, 0, 0))]
    c_specs = [*_seq_specs(tl, wb, t, 7), *_seq_specs(tl, wb, t, 8), tile(6), tile(9), taps(c_conv_w)]
    assert (len(a_specs), len(b_specs), len(c_specs)) == (_N_CONFORMER_IN, _N_GMLP_IN, _N_SHORT_CONV_IN)
    slab = pl.BlockSpec((tl, wb), lambda i: (i, 0))
    in_specs = [*a_specs, *b_specs, *c_specs]
    args = [*[p] * 7, a_conv_w, a_conv_b, a_ln_g, a_ln_b, *[p] * 3, b_ln_g, b_ln_b, ws, bias_full, *[p] * 8, c_conv_w]
    out_specs = [slab, slab, slab]
    out_shape = [jax.ShapeDtypeStruct((t, wb), BF16)] * 3
    aliases = {}
    n_attn_in = 0
    if ctx_attn is not None:
        heads, depth, new_k, new_v = ctx_attn
        assert tl == seq_len
        kv_shape = jax.ShapeDtypeStruct((t // seq_len, depth, seq_len, wb), F32)
        in_specs += [tile(10), tile(11), tile(12), tile(13)]
        args += [p, p, p, p]
        n_attn_in = 4
        if new_k is None:
            kv_out = pl.BlockSpec((None, depth, seq_len, wb), lambda b: (b, 0, 0, 0))
        else:
            kv_out = pl.BlockSpec((None, None, seq_len, wb), lambda b: (b, layer, 0, 0))
            aliases = {len(in_specs): 4, len(in_specs) + 1: 5}
            in_specs += [pl.BlockSpec(memory_space=pl.ANY)] * 2
            args += [new_k, new_v]
            n_attn_in = 6
        out_specs += [slab, kv_out, kv_out]
        out_shape += [jax.ShapeDtypeStruct((t, wb), BF16), kv_shape, kv_shape]
    return pl.pallas_call(
        functools.partial(_seq_branches_kernel, tl=tl, tiles_per_seq=seq_len // tl,
                          attn_heads=ctx_attn[0] if ctx_attn else 0, n_attn_in=n_attn_in, layer=layer),
        grid=(t // tl,),
        in_specs=in_specs,
        out_specs=out_specs,
        out_shape=out_shape,
        input_output_aliases=aliases,
        scratch_shapes=[pltpu.VMEM((tl + 2 * HALO, wb), F32), pltpu.VMEM((tl, wb), F32)],
        compiler_params=_params("arbitrary"),
        name="branches_abcd" if ctx_attn else "branches_abc",
    )(*args)


_NT = (((1,), (1,)), ((), ()))


def _ctx_attn_kernel(q_ref, k_ref, v_ref, gate_ref, *rest, nh, layer):
    o_ref, knew_ref, vnew_ref = rest[-3:]
    if len(rest) == 3:
        knew_ref[...] = jnp.zeros(knew_ref.shape, knew_ref.dtype)
        vnew_ref[...] = jnp.zeros(vnew_ref.shape, vnew_ref.dtype)
        knew_ref, vnew_ref = knew_ref.at[layer], vnew_ref.at[layer]
    knew_ref[...] = k_ref[...].astype(knew_ref.dtype)
    vnew_ref[...] = v_ref[...].astype(vnew_ref.dtype)
    hd = o_ref.shape[1] // nh
    scale = hd ** -0.5
    for h in range(nh):
        cols = slice(h * hd, (h + 1) * hd)
        q = q_ref[:, cols].astype(BF16)
        s = lax.dot_general(q, k_ref[:, cols].astype(BF16), _NT, preferred_element_type=F32) * scale
        e = jnp.exp(s - jnp.max(s, axis=-1, keepdims=True))
        o = jnp.dot(e.astype(BF16), v_ref[:, cols].astype(BF16), preferred_element_type=F32)
        o = o / jnp.sum(e, axis=-1, keepdims=True)
        o_ref[:, cols] = (o * _silu(gate_ref[:, cols].astype(F32))).astype(o_ref.dtype)


def _nbr_attn_kernel(q_ref, k_ref, v_ref, gate_ref, ck_ref, cv_ref, bias_ref, o_ref, oc_ref, mc_ref, lc_ref,
                     *, rows, kr, hd):
    for h in range(o_ref.shape[1] // hd):
        cols = slice(h * hd, (h + 1) * hd)
        _nbr_attn_head(q_ref, k_ref, v_ref, gate_ref, ck_ref, cv_ref, bias_ref.at[h], o_ref, oc_ref, mc_ref, lc_ref,
                       cols, rows, kr)


def _nbr_attn_head(q_ref, k_ref, v_ref, gate_ref, ck_ref, cv_ref, bias_ref, o_ref, oc_ref, mc_ref, lc_ref,
                   cols, rows, kr):
    hd = cols.stop - cols.start
    seq_len = q_ref.shape[0]
    win_r = bias_ref.shape[0] // 2 + 1
    scale = hd ** -0.5 * LOG2E
    ck = ck_ref[:, cols].astype(BF16)
    cv = cv_ref[:, cols].astype(BF16)

    cq = _tile(seq_len, 512)

    def ctx_chunk(ci, carry):
        r0 = pl.multiple_of(ci * cq, cq)
        s = lax.dot_general(q_ref[pl.ds(r0, cq), cols].astype(BF16), ck, _NT, preferred_element_type=F32) * scale
        m = jnp.max(s, axis=-1, keepdims=True)
        e = jnp.exp2(s - m)
        mc_ref[pl.ds(r0, cq), :] = m
        lc_ref[pl.ds(r0, cq), :] = jnp.sum(e, axis=-1, keepdims=True)
        oc_ref[pl.ds(r0, cq), :] = jnp.dot(e.astype(BF16), cv, preferred_element_type=F32)
        return carry

    lax.fori_loop(0, seq_len // cq, ctx_chunk, 0, unroll=True)

    group = ATTN_ROW_GROUP if rows % ATTN_ROW_GROUP == 0 else 1

    def window_starts(gi):
        rs = [gi * group + i for i in range(group)]
        starts = [jnp.clip(r - kr // 2, 0, rows - kr) for r in rs]
        return rs, starts, [pl.multiple_of(s * GRID_W, GRID_W) for s in starts]

    def scores_stage(gi):
        rs, starts, k0s = window_starts(gi)
        out = []
        for r, s, k0 in zip(rs, starts, k0s):
            q0 = pl.multiple_of(r * GRID_W, GRID_W)
            base = win_r - 1 - (r - s)
            bias = jnp.concatenate([bias_ref[base + 2 * t] for t in range(kr // 2)], axis=1)
            out.append(lax.dot_general(q_ref[pl.ds(q0, GRID_W), cols].astype(BF16),
                                       k_ref[pl.ds(k0, kr * GRID_W), cols].astype(BF16), _NT,
                                       preferred_element_type=F32) * scale + bias)
        return tuple(out)

    def softmax_stage(scores):
        out = []
        for sc in scores:
            m = jnp.max(sc, axis=-1, keepdims=True)
            e = jnp.exp2(sc - m)
            out.append((e.astype(BF16), m, jnp.sum(e, axis=-1, keepdims=True)))
        return tuple(out)

    def values_stage(gi, stats):
        rs, _, k0s = window_starts(gi)
        for r, k0, (e, m_l, l_l) in zip(rs, k0s, stats):
            q0 = pl.multiple_of(r * GRID_W, GRID_W)
            o_l = jnp.dot(e, v_ref[pl.ds(k0, kr * GRID_W), cols].astype(BF16), preferred_element_type=F32)
            m_c = mc_ref[pl.ds(q0, GRID_W), :]
            m = jnp.maximum(m_l, m_c)
            a_l = jnp.exp2(m_l - m)
            a_c = jnp.exp2(m_c - m)
            o = (a_l * o_l + a_c * oc_ref[pl.ds(q0, GRID_W), :]) / (a_l * l_l + a_c * lc_ref[pl.ds(q0, GRID_W), :])
            g = gate_ref[pl.ds(q0, GRID_W), cols].astype(F32)
            o_ref[pl.ds(q0, GRID_W), cols] = (o * _silu(g)).astype(o_ref.dtype)

    def pipelined(gi, stats):
        nxt = softmax_stage(scores_stage(gi + 1))
        values_stage(gi, stats)
        return nxt

    n_groups = rows // group
    last = lax.fori_loop(0, n_groups - 1, pipelined, softmax_stage(scores_stage(jnp.int32(0))))
    values_stage(jnp.int32(n_groups - 1), last)


def _branch_d_latent(p, seq_len, cache_k, cache_v, bias_tab, layer):
    t = p.shape[0]
    wb = p.shape[1] // N_IN_SLICES
    nh, n_off = bias_tab.shape[1], bias_tab.shape[2]
    hd = wb // nh
    past = cache_k.shape[2]
    rows = seq_len // GRID_W
    kr = min(n_off // 2 + 1, rows)
    assert kr % 2 == 0
    hp = ATTN_HEADS_PER_STEP if nh % ATTN_HEADS_PER_STEP == 0 else 1
    steps = nh // hp
    col = lambda s: pl.BlockSpec((seq_len, hp * hd), lambda b, h, s=s: (b, s * steps + h))
    ctx = pl.BlockSpec((None, None, past, hp * hd), lambda b, h: (b, layer, 0, h))
    return pl.pallas_call(
        functools.partial(_nbr_attn_kernel, rows=rows, kr=kr, hd=hd),
        grid=(t // seq_len, steps),
        in_specs=[col(10), col(11), col(12), col(13), ctx, ctx,
                  pl.BlockSpec((None, hp, n_off, GRID_W, 2 * GRID_W), lambda b, h: (layer, h, 0, 0, 0))],
        out_specs=pl.BlockSpec((seq_len, hp * hd), lambda b, h: (b, h)),
        out_shape=jax.ShapeDtypeStruct((t, wb), BF16),
        scratch_shapes=[pltpu.VMEM((seq_len, hd), F32), pltpu.VMEM((seq_len, 1), F32), pltpu.VMEM((seq_len, 1), F32)],
        compiler_params=_params("arbitrary", "arbitrary"),
        name="branch_d_neighbourhood_attention",
    )(p, p, p, p, cache_k, cache_v, bias_tab)


def _neighbourhood_bias(rpb):
    win_c = (rpb.shape[3] + 1) // 2
    qc = np.arange(GRID_W)[:, None]
    kc = np.arange(GRID_W)[None, :]
    sc = np.clip(qc - win_c // 2, 0, GRID_W - win_c)
    valid = (kc >= sc) & (kc < sc + win_c)
    select = (kc - qc + win_c - 1 == np.arange(2 * win_c - 1)[:, None, None]) & valid
    cols = jnp.einsum("lhrd,dqk->lhrqk", rpb.astype(F32), jnp.asarray(select, F32), precision=lax.Precision.HIGHEST)
    cols = jnp.where(valid, cols * LOG2E, NEG)
    return jnp.concatenate([cols[:, :, :-1], cols[:, :, 1:]], axis=-1)


def kernel(x_prompt, x_sample, cache_k, cache_v, c, c_ctx, w_ada, b_ada, g_pre, g_post, w_in, a_conv_w, a_conv_b,
           a_ln_g, a_ln_b, b_ln_g, b_ln_b, b_ws, b_bias, c_conv_w, d_rpb, w_out):
    batch, seq, d = x_prompt.shape
    dec_batch, dec_seq, _ = x_sample.shape
    depth = w_in.shape[0]
    wb = d // 4
    nh, hd = cache_k.shape[3], cache_k.shape[4]
    past = cache_k.shape[2]
    assert nh * hd == wb and w_in.shape[2] == N_IN_SLICES * wb and dec_seq % GRID_W == 0

    n_rows = -(-(1 + dec_batch) // 8) * 8
    cond = jnp.zeros((n_rows, d), F32).at[0].set(c_ctx).at[1:1 + dec_batch].set(c)
    mod = _ada_modulation(cond, w_ada, b_ada).reshape(depth, n_rows, 3, 1, d)

    vec = lambda a: a.reshape(depth, 1, a.shape[-1])
    g_pre, g_post, a_conv_b, a_ln_g, a_ln_b, b_ln_g, b_ln_b = map(
        vec, (g_pre, g_post, a_conv_b, a_ln_g, a_ln_b, b_ln_g, b_ln_b))
    ws_b = b_ws.astype(BF16)
    gh = wb // b_ws.shape[1]
    bias_full = jnp.repeat(jnp.swapaxes(b_bias, 1, 2), gh, axis=2)
    bias_tab = _neighbourhood_bias(d_rpb)
    ck = cache_k.reshape(dec_batch, depth, past, wb)
    cv = cache_v.reshape(dec_batch, depth, past, wb)

    xs = [x_prompt.reshape(batch * seq, d), x_sample.reshape(dec_batch * dec_seq, d)]
    seqs = [seq, dec_seq]
    row0 = [0, 1]
    mod_seq = [batch * seq, dec_seq]
    p_dtype = [F32, BF16]
    hn = [_prenorm(xs[i], g_pre, mod, 0, row0[i], mod_seq[i]) for i in range(2)]
    new_k = new_v = None
    w_in_l, w_out_l = w_in[0].astype(BF16), w_out[0].astype(BF16)
    lat_rows = xs[1].shape[0]
    lat_row_tiles = lat_rows // _proj_tiles(lat_rows, d)[0]
    for l in range(depth):
        has_next = l + 1 < depth

        def branches(i, p, ctx_attn=None):
            return _seq_branches(p, seqs[i], l, a_conv_w, a_conv_b, a_ln_g, a_ln_b, b_ln_g, b_ln_b, ws_b, bias_full,
                                 c_conv_w, ctx_attn)

        p = _in_proj(hn[0], w_in_l, p_dtype[0])[0]
        *slabs, new_k, new_v = branches(0, p, (nh, depth, new_k, new_v))
        y = _out_proj(slabs, w_out_l)[0]
        xs[0], hn[0] = _post(xs[0], y, g_post, g_pre, mod, l, row0[0], mod_seq[0], has_next)

        rider = lambda w: (_weight_round_rider(w, l + 1, lat_row_tiles, _proj_tiles(lat_rows, w.shape[2])[1]),) \
            if has_next else ()
        p, *w_in_next = _in_proj(hn[1], w_in_l, p_dtype[1], rider(w_in))
        yd = _branch_d_latent(p, dec_seq, ck, cv, bias_tab, l)
        y, *w_out_next = _out_proj((*branches(1, p), yd), w_out_l, rider(w_out))
        xs[1], hn[1] = _post(xs[1], y, g_post, g_pre, mod, l, row0[1], mod_seq[1], has_next)
        if has_next:
            w_in_l, w_out_l = w_in_next[0], w_out_next[0]
    return (xs[0].reshape(batch, seq, d), xs[1].reshape(dec_batch, dec_seq, d),
            new_k.reshape(batch, depth, seq, nh, hd), new_v.reshape(batch, depth, seq, nh, hd))
```

```python
import collections
import functools

import jax
import jax.numpy as jnp
import numpy as np
from jax import lax
from jax.experimental import pallas as pl
from jax.experimental.pallas import tpu as pltpu

GRID_W = 64
EPS = 1e-6
NEG = -1e30
LOG2E = 1.4426950408889634
N_IN_SLICES = 14
HALO = 16
VMEM_LIMIT_BYTES = 56 * 1024 * 1024
MM_TILE = 1024
SEQ_TILE = 512
NORM_ROWS = 256
CONV_ROWS = 64
ATTN_HEADS_PER_STEP = 2
ATTN_ROW_GROUP = 4
LANES = 128
SUBLANES = 8

F32 = jnp.float32
BF16 = jnp.bfloat16


def _tile(n, target, unit=128):
    if n <= target:
        return n
    t = target - target % unit
    while n % t:
        t -= unit
    return t


def _params(*sem):
    return pltpu.CompilerParams(dimension_semantics=sem, vmem_limit_bytes=VMEM_LIMIT_BYTES)


def _silu(x):
    return x * jax.nn.sigmoid(x)


def _rms(x, g):
    return x * lax.rsqrt(jnp.mean(x * x, axis=-1, keepdims=True) + EPS) * g


def _ln(x, g, b):
    mu = jnp.mean(x, axis=-1, keepdims=True)
    xc = x - mu
    var = jnp.mean(xc * xc, axis=-1, keepdims=True)
    return xc * lax.rsqrt(var + EPS) * g + b


def _ada_kernel(cond_ref, w_ref, b_ref, o_ref):
    a = _silu(cond_ref[...]).astype(BF16)
    o_ref[...] = jnp.dot(a, w_ref[...].astype(BF16), preferred_element_type=F32) + b_ref[...]


def _ada_modulation(cond, w_ada, b_ada):
    depth, d, n = w_ada.shape
    r = cond.shape[0]
    tn = _tile(n, 512)
    return pl.pallas_call(
        _ada_kernel,
        grid=(depth, n // tn),
        in_specs=[pl.BlockSpec((r, d), lambda l, j: (0, 0)),
                  pl.BlockSpec((None, d, tn), lambda l, j: (l, 0, j)),
                  pl.BlockSpec((None, 1, tn), lambda l, j: (l, 0, j))],
        out_specs=pl.BlockSpec((None, r, tn), lambda l, j: (l, 0, j)),
        out_shape=jax.ShapeDtypeStruct((depth, r, n), F32),
        compiler_params=_params("arbitrary", "arbitrary"),
        name="ada_modulation",
    )(cond, w_ada, b_ada.reshape(depth, 1, n))


def _prenorm_kernel(x_ref, g_ref, mod_ref, o_ref):
    y = _rms(x_ref[...], g_ref[...])
    o_ref[...] = (y * (1.0 + mod_ref[1]) + mod_ref[0]).astype(o_ref.dtype)


def _post_kernel(x_ref, y_ref, gpost_ref, mod_ref, *rest, has_next):
    xn = x_ref[...] + mod_ref[2] * _rms(y_ref[...].astype(F32), gpost_ref[...])
    if has_next:
        gpre_ref, modn_ref, xo_ref, hn_ref = rest
        hn_ref[...] = (_rms(xn, gpre_ref[...]) * (1.0 + modn_ref[1]) + modn_ref[0]).astype(hn_ref.dtype)
    else:
        (xo_ref,) = rest
    xo_ref[...] = xn


def _row_tile(t):
    return min(NORM_ROWS, t)


def _vec_spec(layer, width):
    return pl.BlockSpec((None, 1, width), lambda *_: (layer, 0, 0))


def _mod_spec(layer, row0, tiles_per_row, d):
    return pl.BlockSpec((None, None, 3, 1, d), lambda i: (layer, row0 + i // tiles_per_row, 0, 0, 0))


Rider = collections.namedtuple("Rider", "fn in_specs args out_specs out_shape")


def _run_riders(riders, in_refs, out_refs):
    for r in riders:
        n_in, n_out = len(r.in_specs), len(r.out_specs)
        r.fn(*in_refs[:n_in], *out_refs[:n_out])
        in_refs, out_refs = in_refs[n_in:], out_refs[n_out:]


def _prenorm(x, g, mod, layer, row0, seq_len):
    t, d = x.shape
    tm = _row_tile(seq_len)
    return pl.pallas_call(
        _prenorm_kernel,
        grid=(t // tm,),
        in_specs=[pl.BlockSpec((tm, d), lambda i: (i, 0)),
                  _vec_spec(layer, d),
                  _mod_spec(layer, row0, seq_len // tm, d)],
        out_specs=pl.BlockSpec((tm, d), lambda i: (i, 0)),
        out_shape=jax.ShapeDtypeStruct((t, d), BF16),
        compiler_params=_params("arbitrary"),
        name="prenorm",
    )(x, g, mod)


def _post(x, y, g_post, g_pre, mod, layer, row0, seq_len, has_next):
    t, d = x.shape
    tm = _row_tile(seq_len)
    tpr = seq_len // tm
    row = pl.BlockSpec((tm, d), lambda i: (i, 0))
    in_specs = [row, row, _vec_spec(layer, d), _mod_spec(layer, row0, tpr, d)]
    args = [x, y, g_post, mod]
    out_specs = [row]
    out_shape = [jax.ShapeDtypeStruct((t, d), F32)]
    if has_next:
        in_specs += [_vec_spec(layer + 1, d), _mod_spec(layer + 1, row0, tpr, d)]
        args += [g_pre, mod]
        out_specs.append(row)
        out_shape.append(jax.ShapeDtypeStruct((t, d), BF16))
    out = pl.pallas_call(
        functools.partial(_post_kernel, has_next=has_next),
        grid=(t // tm,),
        in_specs=in_specs,
        out_specs=out_specs,
        out_shape=out_shape,
        compiler_params=_params("arbitrary"),
        name="post",
    )(*args)
    return (out[0], out[1]) if has_next else (out[0], None)


def _round_kernel(w_ref, o_ref):
    o_ref[...] = w_ref[...].astype(o_ref.dtype)


def _weight_round_rider(w_all, layer, grid_rows, tn):
    k = w_all.shape[1]
    rows = k // grid_rows
    assert rows * grid_rows == k and rows % (2 * SUBLANES) == 0
    return Rider(_round_kernel,
                 [pl.BlockSpec((None, rows, tn), lambda i, j: (layer, i, j))], [w_all],
                 [pl.BlockSpec((rows, tn), lambda i, j: (i, j))], [jax.ShapeDtypeStruct(w_all.shape[1:], BF16)])


def _rider_lists(riders):
    cat = lambda field: [v for r in riders for v in getattr(r, field)]
    return cat("in_specs"), cat("args"), cat("out_specs"), cat("out_shape")


def _in_proj_kernel(x_ref, w_ref, *rest, riders):
    n_in = sum(len(r.in_specs) for r in riders)
    o_ref = rest[n_in]
    o_ref[...] = jnp.dot(x_ref[...], w_ref[...], preferred_element_type=F32).astype(o_ref.dtype)
    _run_riders(riders, rest[:n_in], rest[n_in + 1:])


def _proj_tiles(t, n):
    return _tile(t, MM_TILE), _tile(n, MM_TILE)


def _in_proj(hn, w, out_dtype, riders=()):
    t, d = hn.shape
    n = w.shape[1]
    tm, tn = _proj_tiles(t, n)
    r_in, r_args, r_out, r_shape = _rider_lists(riders)
    return pl.pallas_call(
        functools.partial(_in_proj_kernel, riders=riders),
        grid=(t // tm, n // tn),
        in_specs=[pl.BlockSpec((tm, d), lambda i, j: (i, 0)), pl.BlockSpec((d, tn), lambda i, j: (0, j)), *r_in],
        out_specs=[pl.BlockSpec((tm, tn), lambda i, j: (i, j)), *r_out],
        out_shape=[jax.ShapeDtypeStruct((t, n), out_dtype), *r_shape],
        compiler_params=_params("arbitrary", "arbitrary"),
        name="in_proj",
    )(hn, w, *r_args)


def _out_proj_kernel(a_ref, b_ref, c_ref, d_ref, w_ref, *rest, riders):
    n_in = sum(len(r.in_specs) for r in riders)
    o_ref, cat_ref = rest[n_in], rest[-1]
    wb = a_ref.shape[1]

    @pl.when(pl.program_id(1) == 0)
    def _():
        for s, ref in enumerate((a_ref, b_ref, c_ref, d_ref)):
            cat_ref[:, s * wb:(s + 1) * wb] = ref[...]

    o_ref[...] = jnp.dot(cat_ref[...], w_ref[...], preferred_element_type=F32).astype(o_ref.dtype)
    _run_riders(riders, rest[:n_in], rest[n_in + 1:-1])


def _out_proj(branches, w, riders=()):
    t, wb = branches[0].shape
    k, n = w.shape
    tm, tn = _proj_tiles(t, n)
    slab = pl.BlockSpec((tm, wb), lambda i, j: (i, 0))
    r_in, r_args, r_out, r_shape = _rider_lists(riders)
    return pl.pallas_call(
        functools.partial(_out_proj_kernel, riders=riders),
        grid=(t // tm, n // tn),
        in_specs=[slab, slab, slab, slab, pl.BlockSpec((k, tn), lambda i, j: (0, j)), *r_in],
        out_specs=[pl.BlockSpec((tm, tn), lambda i, j: (i, j)), *r_out],
        out_shape=[jax.ShapeDtypeStruct((t, n), BF16), *r_shape],
        scratch_shapes=[pltpu.VMEM((tm, k), BF16)],
        compiler_params=_params("arbitrary", "arbitrary"),
        name="out_proj",
    )(*branches, w, *r_args)


def _seq_specs(tl, wb, total_rows, cols):
    r = tl // HALO
    last = total_rows // HALO - 1
    main = pl.BlockSpec((tl, wb), lambda i: (i, cols))
    prev = pl.BlockSpec((HALO, wb), lambda i: (jnp.maximum(i * r - 1, 0), cols))
    nxt = pl.BlockSpec((HALO, wb), lambda i: (jnp.minimum((i + 1) * r, last), cols))
    return main, prev, nxt


def _fill_halo(z_ref, fn, main, prev, nxt, tl, tiles_per_seq):
    i = pl.program_id(0) % tiles_per_seq
    z_ref[0:HALO, :] = jnp.where(i != 0, fn(*[r[...].astype(F32) for r in prev]), 0.0)
    z_ref[HALO:HALO + tl, :] = fn(*[r[...].astype(F32) for r in main])
    z_ref[HALO + tl:, :] = jnp.where(i != tiles_per_seq - 1, fn(*[r[...].astype(F32) for r in nxt]), 0.0)


def _conv_block(z_ref, cw_ref, r0, rc, lanes):
    taps = cw_ref.shape[0]
    first = HALO - (taps - 1) // 2
    win = z_ref[pl.ds(r0, rc + 2 * HALO), lanes]
    acc = None
    for s in range(SUBLANES):
        offs = [o for o in range(first, first + taps) if o % SUBLANES == s]
        if offs:
            rolled = pltpu.roll(win, win.shape[0] - s, 0) if s else win
            for o in offs:
                term = cw_ref[o - first:o - first + 1, lanes] * rolled[o - s:o - s + rc]
                acc = term if acc is None else acc + term
    return acc


def _conformer_kernel(val, val_p, val_n, glu, glu_p, glu_n, gate_ref, cw_ref, cb_ref, g_ref, b_ref, o_ref, z_ref,
                      y_ref, *, tl, tiles_per_seq):
    _fill_halo(z_ref, lambda v, g: v * jax.nn.sigmoid(g), (val, glu), (val_p, glu_p), (val_n, glu_n), tl, tiles_per_seq)
    wb = o_ref.shape[1]
    rc = min(CONV_ROWS, tl)

    def chunk(ci, carry):
        r0 = pl.multiple_of(ci * rc, rc)
        for cb in range(wb // LANES):
            lanes = slice(cb * LANES, (cb + 1) * LANES)
            y_ref[pl.ds(r0, rc), lanes] = _conv_block(z_ref, cw_ref, r0, rc, lanes) + cb_ref[:, lanes]
        y = _silu(_ln(y_ref[pl.ds(r0, rc), :], g_ref[...], b_ref[...]))
        o_ref[pl.ds(r0, rc), :] = (y * _silu(gate_ref[pl.ds(r0, rc), :].astype(F32))).astype(o_ref.dtype)
        return carry

    lax.fori_loop(0, tl // rc, chunk, 0)


def _gmlp_kernel(u_ref, v_ref, gate_ref, g_ref, b_ref, ws_ref, bias_ref, o_ref, *, tl):
    nh, chunk = ws_ref.shape[0], ws_ref.shape[1]
    hd = o_ref.shape[1] // nh
    vn = _ln(v_ref[...].astype(F32), g_ref[...], b_ref[...]).astype(BF16)
    for c in range(tl // chunk):
        rows = slice(c * chunk, (c + 1) * chunk)
        for h in range(nh):
            cols = slice(h * hd, (h + 1) * hd)
            s = jnp.dot(ws_ref[h], vn[rows, cols], preferred_element_type=F32) + bias_ref[:, cols]
            y = u_ref[rows, cols].astype(F32) * s * _silu(gate_ref[rows, cols].astype(F32))
            o_ref[rows, cols] = y.astype(o_ref.dtype)


def _short_conv_kernel(cc, cc_p, cc_n, cx, cx_p, cx_n, cb_ref, gate_ref, cw_ref, o_ref, z_ref, *, tl, tiles_per_seq):
    _fill_halo(z_ref, lambda a, b: a * b, (cc, cx), (cc_p, cx_p), (cc_n, cx_n), tl, tiles_per_seq)
    wb = o_ref.shape[1]
    rc = min(CONV_ROWS, tl)

    def chunk(ci, carry):
        r0 = pl.multiple_of(ci * rc, rc)
        for blk in range(wb // LANES):
            lanes = slice(blk * LANES, (blk + 1) * LANES)
            y = cb_ref[pl.ds(r0, rc), lanes].astype(F32) * _conv_block(z_ref, cw_ref, r0, rc, lanes)
            o_ref[pl.ds(r0, rc), lanes] = (y * _silu(gate_ref[pl.ds(r0, rc), lanes].astype(F32))).astype(o_ref.dtype)
        return carry

    lax.fori_loop(0, tl // rc, chunk, 0)


_N_CONFORMER_IN, _N_GMLP_IN, _N_SHORT_CONV_IN, _N_CTX_ATTN_IN = 11, 7, 9, 6


def _seq_branches_kernel(*refs, tl, tiles_per_seq, attn_heads, n_attn_in, layer):
    a_in, refs = refs[:_N_CONFORMER_IN], refs[_N_CONFORMER_IN:]
    b_in, refs = refs[:_N_GMLP_IN], refs[_N_GMLP_IN:]
    c_in, refs = refs[:_N_SHORT_CONV_IN], refs[_N_SHORT_CONV_IN:]
    d_in, refs = refs[:n_attn_in], refs[n_attn_in:]
    (oa_ref, ob_ref, oc_ref), d_out, (z_ref, y_ref) = refs[:3], refs[3:-2], refs[-2:]
    _conformer_kernel(*a_in, oa_ref, z_ref, y_ref, tl=tl, tiles_per_seq=tiles_per_seq)
    _gmlp_kernel(*b_in, ob_ref, tl=tl)
    _short_conv_kernel(*c_in, oc_ref, z_ref, tl=tl, tiles_per_seq=tiles_per_seq)
    if attn_heads:
        _ctx_attn_kernel(*d_in, *d_out, nh=attn_heads, layer=layer)


def _seq_branches(p, seq_len, layer, a_conv_w, a_conv_b, a_ln_g, a_ln_b, b_ln_g, b_ln_b, ws, bias_full, c_conv_w,
                  ctx_attn=None):
    t = p.shape[0]
    wb = p.shape[1] // N_IN_SLICES
    tl = _tile(seq_len, SEQ_TILE)
    nh, chunk = ws.shape[1], ws.shape[2]
    vec = _vec_spec(layer, wb)
    tile = lambda s: pl.BlockSpec((tl, wb), lambda i: (i, s))
    taps = lambda w: pl.BlockSpec((None, w.shape[1], wb), lambda i: (layer, 0, 0))
    a_specs = [*_seq_specs(tl, wb, t, 0), *_seq_specs(tl, wb, t, 1), tile(2), taps(a_conv_w), vec, vec, vec]
    b_specs = [tile(3), tile(4), tile(5), vec, vec,
               pl.BlockSpec((None, nh, chunk, chunk), lambda i: (layer, 0, 0, 0)),
               pl.BlockSpec((None, chunk, wb), lambda i: (layer, 0, 0))]
    c_specs = [*_seq_specs(tl, wb, t, 7), *_seq_specs(tl, wb, t, 8), tile(6), tile(9), taps(c_conv_w)]
    assert (len(a_specs), len(b_specs), len(c_specs)) == (_N_CONFORMER_IN, _N_GMLP_IN, _N_SHORT_CONV_IN)
    slab = pl.BlockSpec((tl, wb), lambda i: (i, 0))
    in_specs = [*a_specs, *b_specs, *c_specs]
    args = [*[p] * 7, a_conv_w, a_conv_b, a_ln_g, a_ln_b, *[p] * 3, b_ln_g, b_ln_b, ws, bias_full, *[p] * 8, c_conv_w]
    out_specs = [slab, slab, slab]
    out_shape = [jax.ShapeDtypeStruct((t, wb), BF16)] * 3
    aliases = {}
    n_attn_in = 0
    if ctx_attn is not None:
        heads, depth, new_k, new_v = ctx_attn
        assert tl == seq_len
        kv_shape = jax.ShapeDtypeStruct((t // seq_len, depth, seq_len, wb), F32)
        in_specs += [tile(10), tile(11), tile(12), tile(13)]
        args += [p, p, p, p]
        n_attn_in = 4
        if new_k is None:
            kv_out = pl.BlockSpec((None, depth, seq_len, wb), lambda b: (b, 0, 0, 0))
        else:
            kv_out = pl.BlockSpec((None, None, seq_len, wb), lambda b: (b, layer, 0, 0))
            aliases = {len(in_specs): 4, len(in_specs) + 1: 5}
            in_specs += [pl.BlockSpec(memory_space=pl.ANY)] * 2
            args += [new_k, new_v]
            n_attn_in = 6
        out_specs += [slab, kv_out, kv_out]
        out_shape += [jax.ShapeDtypeStruct((t, wb), BF16), kv_shape, kv_shape]
    return pl.pallas_call(
        functools.partial(_seq_branches_kernel, tl=tl, tiles_per_seq=seq_len // tl,
                          attn_heads=ctx_attn[0] if ctx_attn else 0, n_attn_in=n_attn_in, layer=layer),
        grid=(t // tl,),
        in_specs=in_specs,
        out_specs=out_specs,
        out_shape=out_shape,
        input_output_aliases=aliases,
        scratch_shapes=[pltpu.VMEM((tl + 2 * HALO, wb), F32), pltpu.VMEM((tl, wb), F32)],
        compiler_params=_params("arbitrary"),
        name="branches_abcd" if ctx_attn else "branches_abc",
    )(*args)


_NT = (((1,), (1,)), ((), ()))


def _ctx_attn_kernel(q_ref, k_ref, v_ref, gate_ref, *rest, nh, layer):
    o_ref, knew_ref, vnew_ref = rest[-3:]
    if len(rest) == 3:
        knew_ref[...] = jnp.zeros(knew_ref.shape, knew_ref.dtype)
        vnew_ref[...] = jnp.zeros(vnew_ref.shape, vnew_ref.dtype)
        knew_ref, vnew_ref = knew_ref.at[layer], vnew_ref.at[layer]
    knew_ref[...] = k_ref[...].astype(knew_ref.dtype)
    vnew_ref[...] = v_ref[...].astype(vnew_ref.dtype)
    hd = o_ref.shape[1] // nh
    scale = hd ** -0.5
    for h in range(nh):
        cols = slice(h * hd, (h + 1) * hd)
        q = q_ref[:, cols].astype(BF16)
        s = lax.dot_general(q, k_ref[:, cols].astype(BF16), _NT, preferred_element_type=F32) * scale
        e = jnp.exp(s - jnp.max(s, axis=-1, keepdims=True))
        o = jnp.dot(e.astype(BF16), v_ref[:, cols].astype(BF16), preferred_element_type=F32)
        o = o / jnp.sum(e, axis=-1, keepdims=True)
        o_ref[:, cols] = (o * _silu(gate_ref[:, cols].astype(F32))).astype(o_ref.dtype)


def _nbr_attn_kernel(q_ref, k_ref, v_ref, gate_ref, ck_ref, cv_ref, bias_ref, o_ref, oc_ref, mc_ref, lc_ref,
                     *, rows, kr, hd, nh):
    hp = o_ref.shape[1] // hd
    for h in range(hp):
        cols = slice(h * hd, (h + 1) * hd)
        ctx_rows = pl.ds(pl.program_id(1) * hp + h, ck_ref.shape[0] // nh, stride=nh)
        _nbr_attn_head(q_ref, k_ref, v_ref, gate_ref, ck_ref, cv_ref, bias_ref.at[h], o_ref, oc_ref, mc_ref, lc_ref,
                       cols, ctx_rows, rows, kr)


def _nbr_attn_head(q_ref, k_ref, v_ref, gate_ref, ck_ref, cv_ref, bias_ref, o_ref, oc_ref, mc_ref, lc_ref,
                   cols, ctx_rows, rows, kr):
    hd = cols.stop - cols.start
    seq_len = q_ref.shape[0]
    win_r = bias_ref.shape[0] // 2 + 1
    scale = hd ** -0.5 * LOG2E
    ck = ck_ref[ctx_rows, :].astype(BF16)
    cv = cv_ref[ctx_rows, :].astype(BF16)

    cq = _tile(seq_len, 512)

    def ctx_chunk(ci, carry):
        r0 = pl.multiple_of(ci * cq, cq)
        s = lax.dot_general(q_ref[pl.ds(r0, cq), cols].astype(BF16), ck, _NT, preferred_element_type=F32) * scale
        m = jnp.max(s, axis=-1, keepdims=True)
        e = jnp.exp2(s - m)
        mc_ref[pl.ds(r0, cq), :] = m
        lc_ref[pl.ds(r0, cq), :] = jnp.sum(e, axis=-1, keepdims=True)
        oc_ref[pl.ds(r0, cq), :] = jnp.dot(e.astype(BF16), cv, preferred_element_type=F32)
        return carry

    lax.fori_loop(0, seq_len // cq, ctx_chunk, 0, unroll=True)

    group = ATTN_ROW_GROUP if rows % ATTN_ROW_GROUP == 0 else 1

    def window_starts(gi):
        rs = [gi * group + i for i in range(group)]
        starts = [jnp.clip(r - kr // 2, 0, rows - kr) for r in rs]
        return rs, starts, [pl.multiple_of(s * GRID_W, GRID_W) for s in starts]

    def scores_stage(gi):
        rs, starts, k0s = window_starts(gi)
        out = []
        for r, s, k0 in zip(rs, starts, k0s):
            q0 = pl.multiple_of(r * GRID_W, GRID_W)
            base = win_r - 1 - (r - s)
            bias = jnp.concatenate([bias_ref[base + 2 * t] for t in range(kr // 2)], axis=1)
            out.append(lax.dot_general(q_ref[pl.ds(q0, GRID_W), cols].astype(BF16),
                                       k_ref[pl.ds(k0, kr * GRID_W), cols].astype(BF16), _NT,
                                       preferred_element_type=F32) * scale + bias)
        return tuple(out)

    def softmax_stage(scores):
        out = []
        for sc in scores:
            m = jnp.max(sc, axis=-1, keepdims=True)
            e = jnp.exp2(sc - m)
            out.append((e.astype(BF16), m, jnp.sum(e, axis=-1, keepdims=True)))
        return tuple(out)

    def values_stage(gi, stats):
        rs, _, k0s = window_starts(gi)
        for r, k0, (e, m_l, l_l) in zip(rs, k0s, stats):
            q0 = pl.multiple_of(r * GRID_W, GRID_W)
            o_l = jnp.dot(e, v_ref[pl.ds(k0, kr * GRID_W), cols].astype(BF16), preferred_element_type=F32)
            m_c = mc_ref[pl.ds(q0, GRID_W), :]
            m = jnp.maximum(m_l, m_c)
            a_l = jnp.exp2(m_l - m)
            a_c = jnp.exp2(m_c - m)
            o = (a_l * o_l + a_c * oc_ref[pl.ds(q0, GRID_W), :]) / (a_l * l_l + a_c * lc_ref[pl.ds(q0, GRID_W), :])
            g = gate_ref[pl.ds(q0, GRID_W), cols].astype(F32)
            o_ref[pl.ds(q0, GRID_W), cols] = (o * _silu(g)).astype(o_ref.dtype)

    def pipelined(gi, stats):
        nxt = softmax_stage(scores_stage(gi + 1))
        values_stage(gi, stats)
        return nxt

    n_groups = rows // group
    last = lax.fori_loop(0, n_groups - 1, pipelined, softmax_stage(scores_stage(jnp.int32(0))))
    values_stage(jnp.int32(n_groups - 1), last)


def _branch_d_latent(p, seq_len, cache_k, cache_v, bias_tab, layer):
    t = p.shape[0]
    wb = p.shape[1] // N_IN_SLICES
    nh, n_off = bias_tab.shape[1], bias_tab.shape[2]
    hd = wb // nh
    past = cache_k.shape[2]
    rows = seq_len // GRID_W
    kr = min(n_off // 2 + 1, rows)
    assert kr % 2 == 0
    hp = ATTN_HEADS_PER_STEP if nh % ATTN_HEADS_PER_STEP == 0 else 1
    steps = nh // hp
    col = lambda s: pl.BlockSpec((seq_len, hp * hd), lambda b, h, s=s: (b, s * steps + h))
    ctx = pl.BlockSpec((None, None, past, hd), lambda b, h: (b, layer, 0, 0))
    return pl.pallas_call(
        functools.partial(_nbr_attn_kernel, rows=rows, kr=kr, hd=hd, nh=nh),
        grid=(t // seq_len, steps),
        in_specs=[col(10), col(11), col(12), col(13), ctx, ctx,
                  pl.BlockSpec((None, hp, n_off, GRID_W, 2 * GRID_W), lambda b, h: (layer, h, 0, 0, 0))],
        out_specs=pl.BlockSpec((seq_len, hp * hd), lambda b, h: (b, h)),
        out_shape=jax.ShapeDtypeStruct((t, wb), BF16),
        scratch_shapes=[pltpu.VMEM((seq_len, hd), F32), pltpu.VMEM((seq_len, 1), F32), pltpu.VMEM((seq_len, 1), F32)],
        compiler_params=_params("arbitrary", "arbitrary"),
        name="branch_d_neighbourhood_attention",
    )(p, p, p, p, cache_k, cache_v, bias_tab)


def _neighbourhood_bias(rpb):
    win_c = (rpb.shape[3] + 1) // 2
    qc = np.arange(GRID_W)[:, None]
    kc = np.arange(GRID_W)[None, :]
    sc = np.clip(qc - win_c // 2, 0, GRID_W - win_c)
    valid = (kc >= sc) & (kc < sc + win_c)
    select = (kc - qc + win_c - 1 == np.arange(2 * win_c - 1)[:, None, None]) & valid
    cols = jnp.einsum("lhrd,dqk->lhrqk", rpb.astype(F32), jnp.asarray(select, F32), precision=lax.Precision.HIGHEST)
    cols = jnp.where(valid, cols * LOG2E, NEG)
    return jnp.concatenate([cols[:, :, :-1], cols[:, :, 1:]], axis=-1)


def kernel(x_prompt, x_sample, cache_k, cache_v, c, c_ctx, w_ada, b_ada, g_pre, g_post, w_in, a_conv_w, a_conv_b,
           a_ln_g, a_ln_b, b_ln_g, b_ln_b, b_ws, b_bias, c_conv_w, d_rpb, w_out):
    batch, seq, d = x_prompt.shape
    dec_batch, dec_seq, _ = x_sample.shape
    depth = w_in.shape[0]
    wb = d // 4
    nh, hd = cache_k.shape[3], cache_k.shape[4]
    past = cache_k.shape[2]
    assert nh * hd == wb and w_in.shape[2] == N_IN_SLICES * wb and dec_seq % GRID_W == 0

    n_rows = -(-(1 + dec_batch) // 8) * 8
    cond = jnp.zeros((n_rows, d), F32).at[0].set(c_ctx).at[1:1 + dec_batch].set(c)
    mod = _ada_modulation(cond, w_ada, b_ada).reshape(depth, n_rows, 3, 1, d)

    vec = lambda a: a.reshape(depth, 1, a.shape[-1])
    g_pre, g_post, a_conv_b, a_ln_g, a_ln_b, b_ln_g, b_ln_b = map(
        vec, (g_pre, g_post, a_conv_b, a_ln_g, a_ln_b, b_ln_g, b_ln_b))
    ws_b = b_ws.astype(BF16)
    gh = wb // b_ws.shape[1]
    bias_full = jnp.repeat(jnp.swapaxes(b_bias, 1, 2), gh, axis=2)
    bias_tab = _neighbourhood_bias(d_rpb)
    ck = cache_k.reshape(dec_batch, depth, past * nh, hd)
    cv = cache_v.reshape(dec_batch, depth, past * nh, hd)

    xs = [x_prompt.reshape(batch * seq, d), x_sample.reshape(dec_batch * dec_seq, d)]
    seqs = [seq, dec_seq]
    row0 = [0, 1]
    mod_seq = [batch * seq, dec_seq]
    p_dtype = [F32, BF16]
    hn = [_prenorm(xs[i], g_pre, mod, 0, row0[i], mod_seq[i]) for i in range(2)]
    new_k = new_v = None
    w_in_l, w_out_l = w_in[0].astype(BF16), w_out[0].astype(BF16)
    lat_rows = xs[1].shape[0]
    lat_row_tiles = lat_rows // _proj_tiles(lat_rows, d)[0]
    for l in range(depth):
        has_next = l + 1 < depth

        def branches(i, p, ctx_attn=None):
            return _seq_branches(p, seqs[i], l, a_conv_w, a_conv_b, a_ln_g, a_ln_b, b_ln_g, b_ln_b, ws_b, bias_full,
                                 c_conv_w, ctx_attn)

        p = _in_proj(hn[0], w_in_l, p_dtype[0])[0]
        *slabs, new_k, new_v = branches(0, p, (nh, depth, new_k, new_v))
        y = _out_proj(slabs, w_out_l)[0]
        xs[0], hn[0] = _post(xs[0], y, g_post, g_pre, mod, l, row0[0], mod_seq[0], has_next)

        rider = lambda w: (_weight_round_rider(w, l + 1, lat_row_tiles, _proj_tiles(lat_rows, w.shape[2])[1]),) \
            if has_next else ()
        p, *w_in_next = _in_proj(hn[1], w_in_l, p_dtype[1], rider(w_in))
        yd = _branch_d_latent(p, dec_seq, ck, cv, bias_tab, l)
        y, *w_out_next = _out_proj((*branches(1, p), yd), w_out_l, rider(w_out))
        xs[1], hn[1] = _post(xs[1], y, g_post, g_pre, mod, l, row0[1], mod_seq[1], has_next)
        if has_next:
            w_in_l, w_out_l = w_in_next[0], w_out_next[0]
    return (xs[0].reshape(batch, seq, d), xs[1].reshape(dec_batch, dec_seq, d),
            new_k.reshape(batch, depth, seq, nh, hd), new_v.reshape(batch, depth, seq, nh, hd))
```
